```python
import math
import jax, jax.numpy as jnp
from jax import lax
import numpy as np

D_MODEL = 1024
BATCH = 2
SEQ = 8192
DEPTH = 4

GRID_W = 64
CTX_LEN = 256
HEAD_DIM = 64
D_FF = 4 * D_MODEL
NORM_EPS = 1e-6
ROPE_BASE = 10000.0
NEG_INF = -1e30

GROUP_W = D_MODEL // 4
D_MIX = 4 * GROUP_W

NA_HEADS = GROUP_W // HEAD_DIM
WIN_H = 8
WIN_W = 16
NA_QBLK_W = 16
NA_KSPAN_W = 32

SWA_HEADS = GROUP_W // HEAD_DIM
SWA_KV_HEADS = 2
SWA_WINDOW = 128
SWA_BLOCK = 128

MLA_HEADS = 4
MLA_Q_LORA = 256
MLA_KV_LORA = 128
MLA_NOPE = 64
MLA_ROPE = 32
MLA_V = GROUP_W // MLA_HEADS
MLA_BLOCK = 128

SSD_INNER = GROUP_W
SSD_HEAD_DIM = 64
SSD_HEADS = SSD_INNER // SSD_HEAD_DIM
SSD_GROUPS = 2
SSD_STATE = 128
SSD_CONV = 5
SSD_CHUNK = 128
SSD_CONV_CH = SSD_INNER + 2 * SSD_GROUPS * SSD_STATE

NA_IN = 3 * NA_HEADS * HEAD_DIM
SWA_IN = (SWA_HEADS + 2 * SWA_KV_HEADS) * HEAD_DIM
MLA_IN = MLA_Q_LORA + MLA_KV_LORA + MLA_ROPE
SSD_IN = SSD_INNER + SSD_CONV_CH + 2 * SSD_HEADS
N_IN = NA_IN + SWA_IN + MLA_IN + SSD_IN
IN_SPLITS = (NA_IN, NA_IN + SWA_IN, NA_IN + SWA_IN + MLA_IN)

kernel_name = "hybrid_parallel_groups_dit_block"


def rmsnorm(x, g):
    xf = x.astype(jnp.float32)
    y = xf * lax.rsqrt(jnp.mean(xf * xf, axis=-1, keepdims=True) + NORM_EPS)
    return (y * g.astype(jnp.float32)).astype(x.dtype)


def modulate(x, shift, scale):
    return x * (1 + scale) + shift


def axial_angles(n_tok, dim):
    nf = dim // 4
    inv = 1.0 / (ROPE_BASE ** (jnp.arange(nf, dtype=jnp.float32) / nf))
    t = jnp.arange(n_tok)
    row = (t // GRID_W).astype(jnp.float32)
    col = (t % GRID_W).astype(jnp.float32)
    return row[:, None] * inv, col[:, None] * inv


def rope_half(x, ang):
    nf = ang.shape[-1]
    cos = jnp.cos(ang)[:, None, :]
    sin = jnp.sin(ang)[:, None, :]
    x1, x2 = x[..., :nf], x[..., nf:]
    return jnp.concatenate([x1 * cos - x2 * sin, x2 * cos + x1 * sin], axis=-1).astype(x.dtype)


def axial_rope(x, ang):
    ang_r, ang_c = ang
    half = x.shape[-1] // 2
    return jnp.concatenate([rope_half(x[..., :half], ang_r), rope_half(x[..., half:], ang_c)], axis=-1)


def context_attention(q, k, v, scale, sink=None):
    rep = q.shape[2] // k.shape[2]
    k = jnp.repeat(k, rep, axis=2)
    v = jnp.repeat(v, rep, axis=2)
    s = jnp.einsum('bqhd,bkhd->bhqk', q, k).astype(jnp.float32) * scale
    n_keys = s.shape[-1]
    if sink is not None:
        s_sink = jnp.broadcast_to(sink.astype(jnp.float32)[None, :, None, None], s.shape[:-1] + (1,))
        s = jnp.concatenate([s, s_sink], axis=-1)
    p = jax.nn.softmax(s, axis=-1)[..., :n_keys].astype(v.dtype)
    o = jnp.einsum('bhqk,bkhd->bqhd', p, v)
    return o.reshape(o.shape[0], o.shape[1], -1)


def neighbourhood_attention(p_lat, p_ctx, rpb, need_ctx):
    Bsz, S, _ = p_lat.shape
    Lc = p_ctx.shape[1]
    rows = S // GRID_W
    kh = min(WIN_H, rows)
    nb = GRID_W // NA_QBLK_W
    nk = kh * NA_KSPAN_W
    scale = HEAD_DIM ** -0.5
    q, k, v = [t.reshape(Bsz, S, NA_HEADS, HEAD_DIM) for t in jnp.split(p_lat, 3, axis=-1)]
    qc, kc, vc = [t.reshape(Bsz, Lc, NA_HEADS, HEAD_DIM) for t in jnp.split(p_ctx, 3, axis=-1)]
    r = jnp.arange(rows)
    row_idx = jnp.clip(r - kh // 2, 0, rows - kh)[:, None] + jnp.arange(kh)
    j = jnp.arange(nb)
    col_idx = jnp.clip(j * NA_QBLK_W - WIN_W // 2, 0, GRID_W - NA_KSPAN_W)[:, None] + jnp.arange(NA_KSPAN_W)

    def gather(t):
        t = t.reshape(Bsz, rows, GRID_W, NA_HEADS, HEAD_DIM)[:, row_idx][:, :, :, col_idx]
        t = t.transpose(0, 1, 3, 2, 4, 5, 6)
        return t.reshape(Bsz, rows, nb, nk, NA_HEADS, HEAD_DIM)

    kb, vb = gather(k), gather(v)
    qb = q.reshape(Bsz, rows, nb, NA_QBLK_W, NA_HEADS, HEAD_DIM)
    qcol = j[:, None] * NA_QBLK_W + jnp.arange(NA_QBLK_W)
    cstart = jnp.clip(qcol - WIN_W // 2, 0, GRID_W - WIN_W)
    kcol = col_idx[:, None, :]
    valid = (kcol >= cstart[..., None]) & (kcol < cstart[..., None] + WIN_W)
    valid = jnp.broadcast_to(valid[:, :, None, :], (nb, NA_QBLK_W, kh, NA_KSPAN_W)).reshape(nb, NA_QBLK_W, nk)
    dr = row_idx - r[:, None] + (WIN_H - 1)
    dc = jnp.clip(kcol - qcol[..., None] + (WIN_W - 1), 0, 2 * WIN_W - 2)
    bias = rpb[:, dr[:, None, None, :, None], dc[None, :, :, None, :]]
    bias = bias.reshape(NA_HEADS, rows, nb, NA_QBLK_W, nk).transpose(1, 2, 0, 3, 4)
    s = jnp.einsum('brnqhd,brnkhd->brnhqk', qb, kb).astype(jnp.float32) * scale + bias.astype(jnp.float32)
    s = jnp.where(valid[None, None, :, None], s, NEG_INF)
    s_ctx = jnp.einsum('brnqhd,bkhd->brnhqk', qb, kc).astype(jnp.float32) * scale
    p = jax.nn.softmax(jnp.concatenate([s, s_ctx], axis=-1), axis=-1).astype(v.dtype)
    o = (jnp.einsum('brnhqk,brnkhd->brnqhd', p[..., :nk], vb)
         + jnp.einsum('brnhqk,bkhd->brnqhd', p[..., nk:], vc))
    o = o.reshape(Bsz, S, NA_HEADS * HEAD_DIM)
    o_ctx = context_attention(qc, kc, vc, scale) if need_ctx else None
    return o, o_ctx


def window_attention(p_lat, p_ctx, sink, ang, need_ctx):
    Bsz, S, _ = p_lat.shape
    R = SWA_HEADS // SWA_KV_HEADS
    nblk = S // SWA_BLOCK
    scale = HEAD_DIM ** -0.5

    def heads(p):
        L = p.shape[1]
        q, k, v = jnp.split(p, [SWA_HEADS * HEAD_DIM, (SWA_HEADS + SWA_KV_HEADS) * HEAD_DIM], axis=-1)
        return (q.reshape(Bsz, L, SWA_HEADS, HEAD_DIM), k.reshape(Bsz, L, SWA_KV_HEADS, HEAD_DIM),
                v.reshape(Bsz, L, SWA_KV_HEADS, HEAD_DIM))

    q, k, v = heads(p_lat)
    q, k = axial_rope(q, ang), axial_rope(k, ang)
    qc, kc, vc = heads(p_ctx)
    nc = kc.shape[1]
    qb = q.reshape(Bsz, nblk, SWA_BLOCK, SWA_KV_HEADS, R, HEAD_DIM)

    def band(t):
        tp = jnp.pad(t, ((0, 0), (SWA_BLOCK, SWA_BLOCK), (0, 0), (0, 0)))
        tp = tp.reshape(Bsz, nblk + 2, SWA_BLOCK, SWA_KV_HEADS, HEAD_DIM)
        return jnp.concatenate([tp[:, :-2], tp[:, 1:-1], tp[:, 2:]], axis=2)

    kb, vb = band(k), band(v)
    nk = 3 * SWA_BLOCK
    qpos = jnp.arange(nblk)[:, None] * SWA_BLOCK + jnp.arange(SWA_BLOCK)
    kpos = (jnp.arange(nblk)[:, None] - 1) * SWA_BLOCK + jnp.arange(nk)
    valid = ((jnp.abs(qpos[:, :, None] - kpos[:, None, :]) <= SWA_WINDOW)
             & (kpos[:, None, :] >= 0) & (kpos[:, None, :] < S))
    s = jnp.einsum('bnqgrd,bnkgd->bngrqk', qb, kb).astype(jnp.float32) * scale
    s = jnp.where(valid[None, :, None, None], s, NEG_INF)
    s_ctx = jnp.einsum('bnqgrd,bkgd->bngrqk', qb, kc).astype(jnp.float32) * scale
    s_sink = jnp.broadcast_to(sink.astype(jnp.float32).reshape(1, 1, SWA_KV_HEADS, R, 1, 1), s.shape[:-1] + (1,))
    p = jax.nn.softmax(jnp.concatenate([s, s_ctx, s_sink], axis=-1), axis=-1).astype(v.dtype)
    o = (jnp.einsum('bngrqk,bnkgd->bnqgrd', p[..., :nk], vb)
         + jnp.einsum('bngrqk,bkgd->bnqgrd', p[..., nk:nk + nc], vc))
    o = o.reshape(Bsz, S, SWA_HEADS * HEAD_DIM)
    o_ctx = context_attention(qc, kc, vc, scale, sink) if need_ctx else None
    return o, o_ctx


def latent_attention(p_lat, p_ctx, g_q, g_kv, w_uq, w_ukv, ang, need_ctx):
    Bsz, S, _ = p_lat.shape
    scale = (MLA_NOPE + MLA_ROPE) ** -0.5

    def project(p, rotate):
        L = p.shape[1]
        cq, ckv, kr = jnp.split(p, [MLA_Q_LORA, MLA_Q_LORA + MLA_KV_LORA], axis=-1)
        q = (rmsnorm(cq, g_q) @ w_uq).reshape(Bsz, L, MLA_HEADS, MLA_NOPE + MLA_ROPE)
        kv = (rmsnorm(ckv, g_kv) @ w_ukv).reshape(Bsz, L, MLA_HEADS, MLA_NOPE + MLA_V)
        qn, qr = q[..., :MLA_NOPE], q[..., MLA_NOPE:]
        kn, v = kv[..., :MLA_NOPE], kv[..., MLA_NOPE:]
        kr = kr[:, :, None, :]
        if rotate:
            qr, kr = axial_rope(qr, ang), axial_rope(kr, ang)
        return qn, qr, kn, kr, v

    qn, qr, kn, kr, v = project(p_lat, True)
    qnc, qrc, knc, krc, vc = project(p_ctx, False)
    kn_all = jnp.concatenate([knc, kn], axis=1)
    kr_all = jnp.concatenate([krc, kr], axis=1)[:, :, 0]
    v_all = jnp.concatenate([vc, v], axis=1)
    nblk = S // MLA_BLOCK

    def blocks(t):
        return t.reshape(Bsz, nblk, MLA_BLOCK, *t.shape[2:]).swapaxes(0, 1)

    def attend(qs):
        qn_b, qr_b = qs
        s = (jnp.einsum('bqhd,bkhd->bhqk', qn_b, kn_all)
             + jnp.einsum('bqhd,bkd->bhqk', qr_b, kr_all)).astype(jnp.float32) * scale
        p = jax.nn.softmax(s, axis=-1).astype(v_all.dtype)
        return jnp.einsum('bhqk,bkhd->bqhd', p, v_all)

    o = lax.map(attend, (blocks(qn), blocks(qr)))
    o = o.swapaxes(0, 1).reshape(Bsz, S, MLA_HEADS * MLA_V)
    o_ctx = None
    if need_ctx:
        q_full = jnp.concatenate([qnc, qrc], axis=-1)
        k_full = jnp.concatenate([knc, jnp.broadcast_to(krc, knc.shape[:-1] + (MLA_ROPE,))], axis=-1)
        o_ctx = context_attention(q_full, k_full, vc, scale)
    return o, o_ctx


def centred_depthwise_conv(x, w, b):
    L = x.shape[1]
    pad = SSD_CONV // 2
    xp = jnp.pad(x, ((0, 0), (pad, pad), (0, 0)))
    acc = xp[:, 0:L] * w[0]
    for i in range(1, SSD_CONV):
        acc = acc + xp[:, i:i + L] * w[i]
    return jax.nn.silu(acc + b)


def ssd_scan(x, dt, a, bm, cm, h0):
    f32 = jnp.float32
    Bsz, L, H, P = x.shape
    G, N = bm.shape[2], bm.shape[3]
    nc, Q = L // SSD_CHUNK, SSD_CHUNK
    xdt = (x.astype(f32) * dt[..., None]).reshape(Bsz, nc, Q, H, P)
    bh = jnp.repeat(bm.astype(f32), H // G, axis=2).reshape(Bsz, nc, Q, H, N)
    ch = jnp.repeat(cm.astype(f32), H // G, axis=2).reshape(Bsz, nc, Q, H, N)
    cum = jnp.cumsum((dt * a).reshape(Bsz, nc, Q, H), axis=2)
    lower = jnp.tril(jnp.ones((Q, Q), dtype=bool))
    seg = jnp.exp(jnp.where(lower[None, None, :, :, None],
                            cum[:, :, :, None, :] - cum[:, :, None, :, :], -jnp.inf))
    cb = jnp.einsum('bcihn,bcjhn->bcijh', ch, bh) * seg
    y_diag = jnp.einsum('bcijh,bcjhp->bcihp', cb, xdt)
    decay_end = jnp.exp(cum[:, :, -1:, :] - cum)
    states = jnp.einsum('bcjhn,bcjh,bcjhp->bchpn', bh, decay_end, xdt)
    chunk_decay = jnp.exp(cum[:, :, -1, :])

    def step(h, inp):
        s_c, d_c = inp
        return h * d_c[:, :, None, None] + s_c, h

    h_last, h_start = lax.scan(step, h0, (jnp.moveaxis(states, 1, 0), jnp.moveaxis(chunk_decay, 1, 0)))
    h_start = jnp.moveaxis(h_start, 0, 1)
    y_off = jnp.einsum('bcihn,bchpn->bcihp', ch, h_start) * jnp.exp(cum)[..., None]
    y = (y_diag + y_off).reshape(Bsz, L, H, P)
    return y, h_last


def flip_seq(t, rev):
    return jnp.flip(t, axis=1) if rev else t


def ssd_mixer(p_lat, p_ctx, conv_w, conv_b, dt_bias, a_log, d_skip, g_norm, need_ctx):
    f32 = jnp.float32
    a = -jnp.exp(a_log.astype(f32))

    def prepare(p):
        Bsz, L, _ = p.shape
        z, xbc, dt = jnp.split(p, [SSD_INNER, SSD_INNER + SSD_CONV_CH], axis=-1)
        xbc = centred_depthwise_conv(xbc, conv_w, conv_b)
        x, bm, cm = jnp.split(xbc, [SSD_INNER, SSD_INNER + SSD_GROUPS * SSD_STATE], axis=-1)
        dt = jax.nn.softplus(dt.astype(f32).reshape(Bsz, L, 2, SSD_HEADS) + dt_bias.astype(f32))
        return (z, x.reshape(Bsz, L, SSD_HEADS, SSD_HEAD_DIM), bm.reshape(Bsz, L, SSD_GROUPS, SSD_STATE),
                cm.reshape(Bsz, L, SSD_GROUPS, SSD_STATE), dt)

    zl, xl, bl, cl, dtl = prepare(p_lat)
    zc, xc, bc, cc, dtc = prepare(p_ctx)
    Bsz = xl.shape[0]
    skip = d_skip.astype(f32)[:, None]
    y_lat = xl.astype(f32) * skip
    y_ctx = xc.astype(f32) * skip
    h0 = jnp.zeros((Bsz, SSD_HEADS, SSD_HEAD_DIM, SSD_STATE), f32)
    for direction in range(2):
        rev = direction == 1
        yc_d, h_ctx = ssd_scan(flip_seq(xc, rev), flip_seq(dtc[:, :, direction], rev), a[direction],
                               flip_seq(bc, rev), flip_seq(cc, rev), h0)
        yl_d, _ = ssd_scan(flip_seq(xl, rev), flip_seq(dtl[:, :, direction], rev), a[direction],
                           flip_seq(bl, rev), flip_seq(cl, rev), h_ctx)
        y_lat = y_lat + flip_seq(yl_d, rev)
        y_ctx = y_ctx + flip_seq(yc_d, rev)

    def gate_out(y, z):
        Bz, L = y.shape[:2]
        return rmsnorm(y.reshape(Bz, L, SSD_INNER) * jax.nn.silu(z.astype(f32)), g_norm).astype(z.dtype)

    o = gate_out(y_lat, zl)
    o_ctx = gate_out(y_ctx, zc) if need_ctx else None
    return o, o_ctx


def sq_relu_mlp(x, w1, w2):
    return jnp.square(jax.nn.relu(x @ w1)) @ w2


def setup_inputs(seed: int = 0) -> dict:
    key = jax.random.key(seed)
    ks = jax.random.split(key, 26)
    f32 = jnp.float32

    def nrm(k, shape, scale):
        return jax.random.normal(k, shape, f32) * scale

    def gain(k, shape):
        return 1.0 + 0.02 * jax.random.normal(k, shape, f32)

    dt0 = jnp.exp(jax.random.uniform(ks[16], (DEPTH, 2, SSD_HEADS), f32, math.log(1e-3), math.log(1e-1)))
    return {
        "x": nrm(ks[0], (BATCH, SEQ, D_MODEL), 1.0),
        "c": nrm(ks[1], (BATCH, D_MODEL), 1.0),
        "ctx": nrm(ks[2], (BATCH, CTX_LEN, D_MODEL), 1.0),
        "c_ctx": nrm(ks[3], (D_MODEL,), 1.0),
        "w_mod": nrm(ks[4], (DEPTH, D_MODEL, 6 * D_MODEL), 0.5 * D_MODEL ** -0.5),
        "b_mod": nrm(ks[5], (DEPTH, 6 * D_MODEL), 0.01),
        "g_norm1": gain(ks[6], (DEPTH, D_MODEL)),
        "w_in": nrm(ks[7], (DEPTH, D_MODEL, N_IN), D_MODEL ** -0.5),
        "na_rpb": nrm(ks[8], (DEPTH, NA_HEADS, 2 * WIN_H - 1, 2 * WIN_W - 1), 0.1),
        "swa_sink": nrm(ks[9], (DEPTH, SWA_HEADS), 0.5),
        "mla_g_q": gain(ks[10], (DEPTH, MLA_Q_LORA)),
        "mla_g_kv": gain(ks[11], (DEPTH, MLA_KV_LORA)),
        "mla_w_uq": nrm(ks[12], (DEPTH, MLA_Q_LORA, MLA_HEADS * (MLA_NOPE + MLA_ROPE)), MLA_Q_LORA ** -0.5),
        "mla_w_ukv": nrm(ks[13], (DEPTH, MLA_KV_LORA, MLA_HEADS * (MLA_NOPE + MLA_V)), MLA_KV_LORA ** -0.5),
        "ssd_conv_w": nrm(ks[14], (DEPTH, SSD_CONV, SSD_CONV_CH), SSD_CONV ** -0.5),
        "ssd_conv_b": nrm(ks[15], (DEPTH, SSD_CONV_CH), 0.01),
        "ssd_dt_bias": dt0 + jnp.log(-jnp.expm1(-dt0)),
        "ssd_a_log": jnp.log(jax.random.uniform(ks[17], (DEPTH, 2, SSD_HEADS), f32, 1.0, 16.0)),
        "ssd_d": 1.0 + 0.1 * jax.random.normal(ks[18], (DEPTH, SSD_HEADS), f32),
        "ssd_g_norm": gain(ks[19], (DEPTH, SSD_INNER)),
        "w_out": nrm(ks[20], (DEPTH, D_MIX, D_MODEL), D_MIX ** -0.5),
        "g_norm2": gain(ks[21], (DEPTH, D_MODEL)),
        "w_mlp1": nrm(ks[22], (DEPTH, D_MODEL, D_FF), D_MODEL ** -0.5),
        "w_mlp2": nrm(ks[23], (DEPTH, D_FF, D_MODEL), D_FF ** -0.5),
        "g_final": gain(ks[24], (D_MODEL,)),
    }


def reference(x, c, ctx, c_ctx, w_mod, b_mod, g_norm1, w_in, na_rpb, swa_sink, mla_g_q, mla_g_kv,
              mla_w_uq, mla_w_ukv, ssd_conv_w, ssd_conv_b, ssd_dt_bias, ssd_a_log, ssd_d, ssd_g_norm,
              w_out, g_norm2, w_mlp1, w_mlp2, g_final):
    S = x.shape[1]
    ang_swa = axial_angles(S, HEAD_DIM)
    ang_mla = axial_angles(S, MLA_ROPE)
    c_act = jax.nn.silu(c)[:, None, :]
    cc_act = jax.nn.silu(c_ctx)
    h, hc = x, ctx
    for l in range(DEPTH):
        need_ctx = l < DEPTH - 1
        m = jnp.split(c_act @ w_mod[l] + b_mod[l], 6, axis=-1)
        mc = jnp.split(cc_act @ w_mod[l] + b_mod[l], 6, axis=-1)

        p = modulate(rmsnorm(h, g_norm1[l]), m[0], m[1]) @ w_in[l]
        pc = modulate(rmsnorm(hc, g_norm1[l]), mc[0], mc[1]) @ w_in[l]
        pa, pb, pm, pd = jnp.split(p, IN_SPLITS, axis=-1)
        pa_c, pb_c, pm_c, pd_c = jnp.split(pc, IN_SPLITS, axis=-1)
        oa, oa_c = neighbourhood_attention(pa, pa_c, na_rpb[l], need_ctx)
        ob, ob_c = window_attention(pb, pb_c, swa_sink[l], ang_swa, need_ctx)
        om, om_c = latent_attention(pm, pm_c, mla_g_q[l], mla_g_kv[l], mla_w_uq[l], mla_w_ukv[l], ang_mla, need_ctx)
        od, od_c = ssd_mixer(pd, pd_c, ssd_conv_w[l], ssd_conv_b[l], ssd_dt_bias[l], ssd_a_log[l], ssd_d[l],
                             ssd_g_norm[l], need_ctx)
        h = h + m[2] * (jnp.concatenate([oa, ob, om, od], axis=-1) @ w_out[l])

        h = h + m[5] * sq_relu_mlp(modulate(rmsnorm(h, g_norm2[l]), m[3], m[4]), w_mlp1[l], w_mlp2[l])

        if need_ctx:
            hc = hc + mc[2] * (jnp.concatenate([oa_c, ob_c, om_c, od_c], axis=-1) @ w_out[l])
            hc = hc + mc[5] * sq_relu_mlp(modulate(rmsnorm(hc, g_norm2[l]), mc[3], mc[4]), w_mlp1[l], w_mlp2[l])
    return rmsnorm(h, g_final)
```

```python
import functools

import jax
import jax.numpy as jnp
from jax import lax
from jax.experimental import pallas as pl
from jax.experimental.pallas import tpu as pltpu

F32 = jnp.float32
BF16 = jnp.bfloat16

D_MODEL = 1024
GRID_W = 64
HEAD_DIM = 64
D_FF = 4 * D_MODEL
NORM_EPS = 1e-6
ROPE_BASE = 10000.0
NEG_INF = -1e30
GROUP_W = D_MODEL // 4

NA_HEADS = 4
WIN_H = 8
WIN_W = 16
NA_QROWS = 4
NA_KROWS = 12

SWA_HEADS = 4
SWA_KV_HEADS = 2
SWA_WINDOW = 128
SWA_BLOCK = 128

MLA_HEADS = 4
MLA_Q_LORA = 256
MLA_KV_LORA = 128
MLA_NOPE = 64
MLA_ROPE = 32
MLA_V = 64

SSD_INNER = GROUP_W
SSD_HEADS = 4
SSD_GROUPS = 2
SSD_STATE = 128
SSD_CONV = 5
SSD_CHUNK = 128
SSD_CONV_CH = SSD_INNER + 2 * SSD_GROUPS * SSD_STATE
SSD_HALO = 8

NA_IN = 3 * NA_HEADS * HEAD_DIM
SWA_IN = (SWA_HEADS + 2 * SWA_KV_HEADS) * HEAD_DIM
MLA_IN = MLA_Q_LORA + MLA_KV_LORA + MLA_ROPE
SSD_IN = 2 * SSD_INNER + 2 * SSD_GROUPS * SSD_STATE + 2 * SSD_HEADS

LANE = 128
TOK_TILE = 256
FF_TILE = 512
MOD_ROWS = 8
VMEM_LIMIT = 56 * 1024 * 1024

C_PA = 0
C_PB = C_PA + NA_IN
PB_W = SWA_HEADS * LANE + 2 * SWA_KV_HEADS * HEAD_DIM
C_CQ = C_PB + PB_W
C_CKV = C_CQ + MLA_Q_LORA
C_KR = C_CKV + MLA_KV_LORA
C_Z = C_KR + MLA_HEADS * LANE
C_XBC = C_Z + SSD_INNER
C_DT = C_XBC + SSD_CONV_CH
N_EXT = C_DT + LANE


def _dot(a, b):
    return jnp.dot(a, b, preferred_element_type=F32)


def _dot_nt(a, b):
    return lax.dot_general(a, b, (((1,), (1,)), ((), ())), preferred_element_type=F32)


def _dot_exact(a, b):
    return jnp.dot(a, b, preferred_element_type=F32, precision=lax.Precision.HIGHEST)


def _rms(x, g):
    return x * lax.rsqrt(jnp.mean(x * x, axis=-1, keepdims=True) + NORM_EPS) * g


def _silu(x):
    return x * jax.nn.sigmoid(x)


def _rope(x, cos, sin, half):
    n = x.shape[0]
    lane = lax.broadcasted_iota(jnp.int32, (n, LANE), 1)
    first = (lane % (2 * half)) < half
    outs = []
    for c in range(x.shape[1] // LANE):
        xc = x[:, c * LANE:(c + 1) * LANE]
        rot = jnp.where(first, pltpu.roll(xc, LANE - half, 1), pltpu.roll(xc, half, 1))
        outs.append(xc * cos + rot * sin)
    return outs[0] if len(outs) == 1 else jnp.concatenate(outs, axis=1)


def _params(n_axes):
    return pltpu.CompilerParams(dimension_semantics=("arbitrary",) * n_axes, vmem_limit_bytes=VMEM_LIMIT)


def _mod_kernel(c_ref, w_ref, b_ref, o_ref):
    o_ref[...] = _dot_exact(_silu(c_ref[...]), w_ref[...]) + b_ref[...]


def _modulation(cvec, w_mod, b_mod):
    depth, d, n6 = w_mod.shape
    tn = 1536
    return pl.pallas_call(
        _mod_kernel,
        grid=(depth, n6 // tn),
        in_specs=[
            pl.BlockSpec((MOD_ROWS, d), lambda l, j: (0, 0)),
            pl.BlockSpec((None, d, tn), lambda l, j: (l, 0, j)),
            pl.BlockSpec((None, 1, tn), lambda l, j: (l, 0, j)),
        ],
        out_specs=pl.BlockSpec((None, MOD_ROWS, tn), lambda l, j: (l, 0, j)),
        out_shape=jax.ShapeDtypeStruct((depth, MOD_ROWS, n6), F32),
        compiler_params=_params(2),
        name="modulation",
    )(cvec, w_mod, b_mod.reshape(depth, 1, n6))


def _mod_row(t, tiles_per_batch):
    return jnp.where(t % tiles_per_batch == 0, 0, 1 + t // tiles_per_batch)


def _inproj_kernel(h_ref, mod_ref, g1_ref, w_ref, scos_ref, ssin_ref, mcos_ref, msin_ref,
                   gq_ref, gkv_ref, wuq_ref, wk_ref, wv_ref,
                   pa_ref, pb_ref, qm_ref, km_ref, vm_ref, pz_ref, pxbc_ref, dt_ref):
    xn = _rms(h_ref[...], g1_ref[...])
    xm = (xn * (1.0 + mod_ref[1:2, :]) + mod_ref[0:1, :]).astype(BF16)

    def proj(lo, hi):
        return _dot(xm, w_ref[:, lo:hi])

    pa_ref[...] = proj(C_PA, C_PB).astype(BF16)

    n_qk = (SWA_HEADS + 1) * LANE
    qk = _rope(proj(C_PB, C_PB + n_qk), scos_ref[...], ssin_ref[...], HEAD_DIM // 4)
    pb_ref[:, 0:n_qk] = qk.astype(BF16)
    pb_ref[:, n_qk:PB_W] = proj(C_PB + n_qk, C_CQ).astype(BF16)

    mcos = mcos_ref[...]
    msin = msin_ref[...]
    cq = _rms(proj(C_CQ, C_CKV), gq_ref[...]).astype(BF16)
    qm_ref[...] = _rope(_dot(cq, wuq_ref[...]), mcos, msin, MLA_ROPE // 4).astype(BF16)
    ckv = _rms(proj(C_CKV, C_KR), gkv_ref[...]).astype(BF16)
    kr = _rope(proj(C_KR, C_Z), mcos, msin, MLA_ROPE // 4)
    km_ref[...] = (_dot(ckv, wk_ref[...]) + kr).astype(BF16)
    vm_ref[...] = _dot(ckv, wv_ref[...]).astype(BF16)

    pz_ref[...] = proj(C_Z, C_XBC)
    pxbc_ref[...] = proj(C_XBC, C_DT)
    dt_ref[...] = proj(C_DT, N_EXT)


def _inproj(h, mods, g1, w_ext, tabs, gq, gkv, wuq, wk, wv, tiles_per_batch):
    n_tok = h.shape[0]
    tm = TOK_TILE
    scos, ssin, mcos, msin = tabs
    tok = lambda n: pl.BlockSpec((tm, n), lambda t: (t, 0))
    full = lambda a: pl.BlockSpec(a.shape, lambda t: (0,) * a.ndim)
    tab = pl.BlockSpec((tm, LANE), lambda t: (t % tiles_per_batch, 0))
    widths = (NA_IN, PB_W, MLA_HEADS * LANE, MLA_HEADS * LANE, MLA_HEADS * MLA_V, SSD_INNER, SSD_CONV_CH, LANE)
    dtypes = (BF16, BF16, BF16, BF16, BF16, F32, F32, F32)
    return pl.pallas_call(
        _inproj_kernel,
        grid=(n_tok // tm,),
        in_specs=[
            tok(D_MODEL),
            pl.BlockSpec((None, 6, D_MODEL), lambda t: (_mod_row(t, tiles_per_batch), 0, 0)),
            full(g1), full(w_ext), tab, tab, tab, tab, full(gq), full(gkv), full(wuq), full(wk), full(wv),
        ],
        out_specs=[tok(n) for n in widths],
        out_shape=[jax.ShapeDtypeStruct((n_tok, n), dt) for n, dt in zip(widths, dtypes)],
        compiler_params=_params(1),
        name="inproj",
    )(h, mods, g1, w_ext, scos, ssin, mcos, msin, gq, gkv, wuq, wk, wv)


def _na_kernel(q_ref, k_ref, v_ref, tab_ref, o_ref, *, rows, lc):
    i = pl.program_id(1)
    scale = HEAD_DIM ** -0.5
    nq = q_ref.shape[0]
    q = q_ref[...]
    lane = lax.broadcasted_iota(jnp.int32, (nq, NA_HEADS * HEAD_DIM), 1)
    k_ctx = k_ref[0:lc, :]
    v_ctx = v_ref[0:lc, :]

    def attend(k_win, v_win):
        out = jnp.zeros((nq, NA_HEADS * HEAD_DIM), F32)
        for h in range(NA_HEADS):
            in_head = (lane // HEAD_DIM) == h
            qh = jnp.where(in_head, q, jnp.zeros_like(q))
            s_c = _dot_nt(qh, k_ctx) * scale
            m = jnp.max(s_c, axis=-1, keepdims=True)
            if k_win is not None:
                s_w = _dot_nt(qh, k_win) * scale + tab_ref[h]
                m = jnp.maximum(m, jnp.max(s_w, axis=-1, keepdims=True))
                p_w = jnp.exp(s_w - m)
            p_c = jnp.exp(s_c - m)
            den = jnp.sum(p_c, axis=-1, keepdims=True)
            o = _dot(p_c.astype(BF16), v_ctx)
            if k_win is not None:
                den = den + jnp.sum(p_w, axis=-1, keepdims=True)
                o = o + _dot(p_w.astype(BF16), v_win)
            out = jnp.where(in_head, o / den, out)
        o_ref[...] = out.astype(o_ref.dtype)

    @pl.when(i == 0)
    def _():
        attend(None, None)

    @pl.when(i > 0)
    def _():
        r0 = (i - 1) * NA_QROWS
        ws = jnp.clip(r0 - WIN_H // 2, 0, rows - NA_KROWS)
        start = pl.multiple_of(lc + ws * GRID_W, GRID_W)
        nk = NA_KROWS * GRID_W
        attend(k_ref[pl.ds(start, nk), :], v_ref[pl.ds(start, nk), :])


def _na_attention(pa, table, rows, lc):
    bsz, n_tok, _ = pa.shape
    nq = NA_QROWS * GRID_W
    groups = rows // NA_QROWS
    w = NA_HEADS * HEAD_DIM

    def cfg(b, i):
        g = i - 1
        return (jnp.where(g <= 0, 0, jnp.where(g == groups - 1, 2, 1)), 0, 0, 0)

    return pl.pallas_call(
        functools.partial(_na_kernel, rows=rows, lc=lc),
        grid=(bsz, 1 + groups),
        in_specs=[
            pl.BlockSpec((None, nq, w), lambda b, i: (b, i, 0)),
            pl.BlockSpec((None, n_tok, w), lambda b, i: (b, 0, 1)),
            pl.BlockSpec((None, n_tok, w), lambda b, i: (b, 0, 2)),
            pl.BlockSpec((None, NA_HEADS, nq, NA_KROWS * GRID_W), cfg),
        ],
        out_specs=pl.BlockSpec((None, nq, w), lambda b, i: (b, i, 0)),
        out_shape=jax.ShapeDtypeStruct((bsz, n_tok, w), BF16),
        compiler_params=_params(2),
        name="na_attention",
    )(pa, pa, pa, table)


def _na_bias_table(rpb, rows):
    groups = rows // NA_QROWS
    i = jnp.arange(NA_QROWS)
    j = jnp.arange(NA_KROWS)
    col = jnp.arange(GRID_W)
    tabs = []
    for g in (0, 1, groups - 1):
        r = g * NA_QROWS + i
        kr = jnp.clip(g * NA_QROWS - WIN_H // 2, 0, rows - NA_KROWS) + j
        rstart = jnp.clip(r - WIN_H // 2, 0, rows - WIN_H)
        ok_r = (kr[None, :] >= rstart[:, None]) & (kr[None, :] < rstart[:, None] + WIN_H)
        cstart = jnp.clip(col - WIN_W // 2, 0, GRID_W - WIN_W)
        ok_c = (col[None, :] >= cstart[:, None]) & (col[None, :] < cstart[:, None] + WIN_W)
        dr = jnp.clip(kr[None, :] - r[:, None] + (WIN_H - 1), 0, 2 * WIN_H - 2)
        dc = jnp.clip(col[None, :] - col[:, None] + (WIN_W - 1), 0, 2 * WIN_W - 2)
        bias = rpb[:, dr[:, None, :, None], dc[None, :, None, :]]
        ok = ok_r[:, None, :, None] & ok_c[None, :, None, :]
        bias = jnp.where(ok[None], bias, NEG_INF)
        tabs.append(bias.reshape(NA_HEADS, NA_QROWS * GRID_W, NA_KROWS * GRID_W))
    return jnp.stack(tabs)


def _swa_kernel(sink_ref, q_ref, k_ref, v_ref, o_ref, *, nblk, lc):
    i = pl.program_id(1)
    scale = HEAD_DIM ** -0.5
    blk = SWA_BLOCK
    rep = SWA_HEADS // SWA_KV_HEADS
    k_ctx = k_ref[0:lc, :]
    v_ctx = v_ref[0:lc, :]
    row = lax.broadcasted_iota(jnp.int32, (rep * blk, 1), 0)

    def attend(k_win, v_win, valid):
        outs = []
        for g in range(SWA_KV_HEADS):
            qq = jnp.concatenate([q_ref[:, (rep * g + r) * LANE:(rep * g + r + 1) * LANE] for r in range(rep)], axis=0)
            sink = jnp.where(row < blk, sink_ref[rep * g], sink_ref[rep * g + 1])
            s_c = _dot_nt(qq, k_ctx) * scale
            m = jnp.maximum(jnp.max(s_c, axis=-1, keepdims=True), sink)
            if k_win is not None:
                s_w = jnp.where(valid, _dot_nt(qq, k_win) * scale, NEG_INF)
                m = jnp.maximum(m, jnp.max(s_w, axis=-1, keepdims=True))
                p_w = jnp.exp(s_w - m)
            p_c = jnp.exp(s_c - m)
            den = jnp.sum(p_c, axis=-1, keepdims=True) + jnp.exp(sink - m)
            o = _dot(p_c.astype(BF16), v_ctx)
            if k_win is not None:
                den = den + jnp.sum(p_w, axis=-1, keepdims=True)
                o = o + _dot(p_w.astype(BF16), v_win)
            outs.append(o / den)
        lane = lax.broadcasted_iota(jnp.int32, (blk, LANE), 1)
        lo = lane < HEAD_DIM
        o_ref[:, 0:LANE] = jnp.where(lo, outs[0][0:blk], outs[1][0:blk]).astype(o_ref.dtype)
        o_ref[:, LANE:2 * LANE] = jnp.where(lo, outs[0][blk:2 * blk], outs[1][blk:2 * blk]).astype(o_ref.dtype)

    n_ctx_blk = lc // blk

    @pl.when(i < n_ctx_blk)
    def _():
        attend(None, None, None)

    @pl.when(i >= n_ctx_blk)
    def _():
        n = i - n_ctx_blk
        wb = jnp.clip(n - 1, 0, nblk - 3)
        start = pl.multiple_of(lc + wb * blk, blk)
        iq = lax.broadcasted_iota(jnp.int32, (rep * blk, 3 * blk), 0) % blk
        ik = lax.broadcasted_iota(jnp.int32, (rep * blk, 3 * blk), 1)
        dist = (n - wb) * blk + iq - ik
        valid = jnp.abs(dist) <= SWA_WINDOW
        attend(k_ref[pl.ds(start, 3 * blk), :], v_ref[pl.ds(start, 3 * blk), :], valid)


def _swa_attention(pb, sink, lc):
    bsz, n_tok, _ = pb.shape
    blk = SWA_BLOCK
    nblk = (n_tok - lc) // blk
    qw = SWA_HEADS * LANE
    kw = SWA_KV_HEADS * HEAD_DIM
    return pl.pallas_call(
        functools.partial(_swa_kernel, nblk=nblk, lc=lc),
        grid=(bsz, n_tok // blk),
        in_specs=[
            pl.BlockSpec(memory_space=pltpu.SMEM),
            pl.BlockSpec((None, blk, qw), lambda b, i: (b, i, 0)),
            pl.BlockSpec((None, n_tok, kw), lambda b, i: (b, 0, qw // kw)),
            pl.BlockSpec((None, n_tok, kw), lambda b, i: (b, 0, qw // kw + 1)),
        ],
        out_specs=pl.BlockSpec((None, blk, SWA_HEADS * HEAD_DIM), lambda b, i: (b, i, 0)),
        out_shape=jax.ShapeDtypeStruct((bsz, n_tok, SWA_HEADS * HEAD_DIM), BF16),
        compiler_params=_params(2),
        name="swa_attention",
    )(sink, pb, pb, pb)


def _mla_kernel(q_ref, k_ref, v_ref, o_ref, *, n_tok, lc, tk):
    i = pl.program_id(1)
    scale = (MLA_NOPE + MLA_ROPE) ** -0.5
    tq = q_ref.shape[0]
    wv = MLA_HEADS * MLA_V
    lane = lax.broadcasted_iota(jnp.int32, (tq, wv), 1)

    def attend(n_chunks, chunk):
        out = jnp.zeros((tq, wv), F32)
        for h in range(MLA_HEADS):
            qh = q_ref[:, h * LANE:(h + 1) * LANE]

            def body(c, carry):
                m, den, acc = carry
                st = pl.multiple_of(c * chunk, chunk)
                s = _dot_nt(qh, k_ref[pl.ds(st, chunk), h * LANE:(h + 1) * LANE]) * scale
                m_new = jnp.maximum(m, jnp.max(s, axis=-1, keepdims=True))
                alpha = jnp.exp(m - m_new)
                p = jnp.exp(s - m_new)
                den = alpha * den + jnp.sum(p, axis=-1, keepdims=True)
                acc = alpha * acc + _dot(p.astype(BF16), v_ref[pl.ds(st, chunk), :])
                return m_new, den, acc

            init = (jnp.full((tq, 1), NEG_INF, F32), jnp.zeros((tq, 1), F32), jnp.zeros((tq, wv), F32))
            _, den, acc = lax.fori_loop(0, n_chunks, body, init)
            out = jnp.where((lane // MLA_V) == h, acc / den, out)
        o_ref[...] = out.astype(o_ref.dtype)

    @pl.when(i == 0)
    def _():
        attend(1, lc)

    @pl.when(i > 0)
    def _():
        attend(n_tok // tk, tk)


def _mla_attention(qm, km, vm, lc):
    bsz, n_tok, qw = qm.shape
    tq = TOK_TILE
    assert lc == tq
    tk = next(c for c in (768, 512, 256) if n_tok % c == 0)
    wv = MLA_HEADS * MLA_V
    return pl.pallas_call(
        functools.partial(_mla_kernel, n_tok=n_tok, lc=lc, tk=tk),
        grid=(bsz, n_tok // tq),
        in_specs=[
            pl.BlockSpec((None, tq, qw), lambda b, i: (b, i, 0)),
            pl.BlockSpec((None, n_tok, qw), lambda b, i: (b, 0, 0)),
            pl.BlockSpec((None, n_tok, wv), lambda b, i: (b, 0, 0)),
        ],
        out_specs=pl.BlockSpec((None, tq, wv), lambda b, i: (b, i, 0)),
        out_shape=jax.ShapeDtypeStruct((bsz, n_tok, wv), BF16),
        compiler_params=_params(2),
        name="mla_attention",
    )(qm, km, vm)


def _ssd_chunk(step, reverse, n_ctx, n_chunks):
    if not reverse:
        return step
    return jnp.where(step < n_ctx, n_ctx - 1 - step, n_chunks + n_ctx - 1 - step)


def _expand_heads(v, base):
    q = v.shape[0]
    lane = lax.broadcasted_iota(jnp.int32, (q, SSD_INNER), 1)
    out = jnp.broadcast_to(v[:, base:base + 1], (q, SSD_INNER))
    for h in range(1, SSD_HEADS):
        out = jnp.where(lane >= h * HEAD_DIM, jnp.broadcast_to(v[:, base + h:base + h + 1], (q, SSD_INNER)), out)
    return out


def _ssd_kernel(*refs, reverse, n_ctx, n_chunks):
    if reverse:
        (z_ref, xbc_ref, prev_ref, next_ref, dt_ref, cw_ref, cb_ref, dtb_ref, alog_ref,
         yf_ref, skip_ref, gn_ref, o_ref, ext_ref, hs_ref) = refs
    else:
        (xbc_ref, prev_ref, next_ref, dt_ref, cw_ref, cb_ref, dtb_ref, alog_ref, o_ref, ext_ref, hs_ref) = refs
    step = pl.program_id(1)
    chunk = _ssd_chunk(step, reverse, n_ctx, n_chunks)
    q = SSD_CHUNK
    pad = SSD_CONV // 2
    base = SSD_HEADS if reverse else 0
    last = 0 if reverse else q - 1

    @pl.when(step == 0)
    def _():
        hs_ref[...] = jnp.zeros_like(hs_ref)

    has_prev = jnp.logical_and(chunk != 0, chunk != n_ctx)
    has_next = jnp.logical_and(chunk != n_ctx - 1, chunk != n_chunks - 1)
    ext_ref[0:SSD_HALO, :] = jnp.where(has_prev, prev_ref[...], 0.0)
    ext_ref[SSD_HALO:SSD_HALO + q, :] = xbc_ref[...]
    ext_ref[SSD_HALO + q:, :] = jnp.where(has_next, next_ref[...], 0.0)
    acc = ext_ref[pl.ds(SSD_HALO - pad, q), :] * cw_ref[0:1, :]
    for t in range(1, SSD_CONV):
        acc = acc + ext_ref[pl.ds(SSD_HALO - pad + t, q), :] * cw_ref[t:t + 1, :]
    xbc = _silu(acc + cb_ref[...])
    x = xbc[:, 0:SSD_INNER]

    dt_in = dt_ref[...] + dtb_ref[...]
    dt = jnp.maximum(dt_in, 0.0) + jnp.log1p(jnp.exp(-jnp.abs(dt_in)))
    da = dt * (-jnp.exp(alog_ref[...]))
    ii = lax.broadcasted_iota(jnp.int32, (q, q), 0)
    jj = lax.broadcasted_iota(jnp.int32, (q, q), 1)
    lower = (jj <= ii).astype(F32)
    upper = (jj >= ii).astype(F32)
    tri, tri_t = (upper, lower) if reverse else (lower, upper)
    cum = _dot_exact(tri, da)
    cum_t = _dot_exact(da.T, tri_t)
    causal = (jj >= ii) if reverse else (jj <= ii)

    dtx = _expand_heads(dt, base)
    cum_x = _expand_heads(cum, base)
    xdt = x * dtx
    xdt_b = xdt.astype(BF16)
    cum_last = cum_x[last:last + 1, :]
    xdtd_b = (xdt * jnp.exp(cum_last - cum_x)).astype(BF16)
    lane = lax.broadcasted_iota(jnp.int32, (q, SSD_INNER), 1)
    hs = hs_ref[...]
    hs_b = hs.astype(BF16)

    y = jnp.zeros((q, SSD_INNER), F32)
    y_off = []
    new_state = []
    for g in range(SSD_GROUPS):
        bm = xbc[:, SSD_INNER + g * SSD_STATE:SSD_INNER + (g + 1) * SSD_STATE]
        cm = xbc[:, SSD_INNER + (SSD_GROUPS + g) * SSD_STATE:SSD_INNER + (SSD_GROUPS + g + 1) * SSD_STATE]
        cm_b = cm.astype(BF16)
        cb = _dot_nt(cm_b, bm.astype(BF16))
        for h in range(g * SSD_HEADS // SSD_GROUPS, (g + 1) * SSD_HEADS // SSD_GROUPS):
            col = cum[:, base + h:base + h + 1]
            rowv = cum_t[base + h:base + h + 1, :]
            seg = jnp.exp(jnp.where(causal, col - rowv, -jnp.inf))
            yd = _dot((cb * seg).astype(BF16), xdt_b)
            y = jnp.where((lane // HEAD_DIM) == h, yd, y)
        y_off.append(_dot(cm_b, hs_b))
        new_state.append(_dot(bm.T.astype(BF16), xdtd_b))
    half = lane < SSD_INNER // SSD_GROUPS
    y = y + jnp.where(half, y_off[0], y_off[1]) * jnp.exp(cum_x)
    hs_ref[...] = hs * jnp.exp(cum_last) + jnp.where(half, new_state[0], new_state[1])

    if reverse:
        y = x * skip_ref[...] + yf_ref[...] + y
        o_ref[...] = _rms(y * _silu(z_ref[...]), gn_ref[...]).astype(o_ref.dtype)
    else:
        o_ref[...] = y


def _ssd_scan(pz, pxbc, dtp, conv_w, conv_b, dt_bias, a_log, lc, reverse, yf=None, skip=None, gnorm=None):
    bsz, n_tok, _ = pxbc.shape
    q = SSD_CHUNK
    n_chunks = n_tok // q
    n_ctx = lc // q
    per = q // SSD_HALO
    chunk = functools.partial(_ssd_chunk, reverse=reverse, n_ctx=n_ctx, n_chunks=n_chunks)
    tok = lambda n: pl.BlockSpec((None, q, n), lambda b, s: (b, chunk(s), 0))
    full = lambda a: pl.BlockSpec(a.shape, lambda b, s: (0,) * a.ndim)
    prev = pl.BlockSpec((None, SSD_HALO, SSD_CONV_CH), lambda b, s: (b, jnp.maximum(chunk(s) * per - 1, 0), 0))
    nxt = pl.BlockSpec((None, SSD_HALO, SSD_CONV_CH),
                       lambda b, s: (b, jnp.minimum((chunk(s) + 1) * per, n_tok // SSD_HALO - 1), 0))
    in_specs = [tok(SSD_CONV_CH), prev, nxt, tok(LANE), full(conv_w), full(conv_b), full(dt_bias), full(a_log)]
    args = [pxbc, pxbc, pxbc, dtp, conv_w, conv_b, dt_bias, a_log]
    if reverse:
        in_specs = [tok(SSD_INNER)] + in_specs + [tok(SSD_INNER), full(skip), full(gnorm)]
        args = [pz] + args + [yf, skip, gnorm]
    return pl.pallas_call(
        functools.partial(_ssd_kernel, reverse=reverse, n_ctx=n_ctx, n_chunks=n_chunks),
        grid=(bsz, n_chunks),
        in_specs=in_specs,
        out_specs=tok(SSD_INNER),
        out_shape=jax.ShapeDtypeStruct((bsz, n_tok, SSD_INNER), BF16 if reverse else F32),
        scratch_shapes=[pltpu.VMEM((q + 2 * SSD_HALO, SSD_CONV_CH), F32), pltpu.VMEM((SSD_STATE, SSD_INNER), F32)],
        compiler_params=_params(2),
        name="ssd_bwd" if reverse else "ssd_fwd",
    )(*args)


def _outmlp_kernel(h_ref, oa_ref, ob_ref, om_ref, od_ref, mod_ref, wo_ref, g2_ref, w1_ref, w2_ref, o_ref):
    mix = jnp.concatenate([oa_ref[...], ob_ref[...], om_ref[...], od_ref[...]], axis=1)
    h1 = h_ref[...] + mod_ref[2:3, :] * _dot(mix, wo_ref[...])
    xm = (_rms(h1, g2_ref[...]) * (1.0 + mod_ref[4:5, :]) + mod_ref[3:4, :]).astype(BF16)
    acc = jnp.zeros(h1.shape, F32)
    for c in range(D_FF // FF_TILE):
        a = jnp.maximum(_dot(xm, w1_ref[:, c * FF_TILE:(c + 1) * FF_TILE]), 0.0)
        acc = acc + _dot((a * a).astype(BF16), w2_ref[c * FF_TILE:(c + 1) * FF_TILE, :])
    o_ref[...] = h1 + mod_ref[5:6, :] * acc


def _outmlp(h, oa, ob, om, od, mods, wo, g2, w1, w2, tiles_per_batch):
    n_tok = h.shape[0]
    tm = TOK_TILE
    tok = lambda n: pl.BlockSpec((tm, n), lambda t: (t, 0))
    full = lambda a: pl.BlockSpec(a.shape, lambda t: (0,) * a.ndim)
    return pl.pallas_call(
        _outmlp_kernel,
        grid=(n_tok // tm,),
        in_specs=[
            tok(D_MODEL), tok(GROUP_W), tok(GROUP_W), tok(GROUP_W), tok(GROUP_W),
            pl.BlockSpec((None, 6, D_MODEL), lambda t: (_mod_row(t, tiles_per_batch), 0, 0)),
            full(wo), full(g2), full(w1), full(w2),
        ],
        out_specs=tok(D_MODEL),
        out_shape=jax.ShapeDtypeStruct((n_tok, D_MODEL), F32),
        compiler_params=_params(1),
        name="outproj_mlp",
    )(h, oa, ob, om, od, mods, wo, g2, w1, w2)


def _final_norm_kernel(h_ref, g_ref, o_ref):
    o_ref[...] = _rms(h_ref[...], g_ref[...])


def _final_norm(h, g, lc):
    bsz, n_tok, d = h.shape
    tm = TOK_TILE
    off = lc // tm
    return pl.pallas_call(
        _final_norm_kernel,
        grid=(bsz, (n_tok - lc) // tm),
        in_specs=[pl.BlockSpec((None, tm, d), lambda b, t: (b, t + off, 0)), pl.BlockSpec((1, d), lambda b, t: (0, 0))],
        out_specs=pl.BlockSpec((None, tm, d), lambda b, t: (b, t, 0)),
        out_shape=jax.ShapeDtypeStruct((bsz, n_tok - lc, d), F32),
        compiler_params=_params(2),
        name="final_norm",
    )(h, g)


def _rope_tables(s, lc):
    t = jnp.arange(s)
    pos = (t // GRID_W).astype(F32), (t % GRID_W).astype(F32)
    lane = jnp.arange(LANE)

    def table(d, width, active):
        nf = width // 4
        inv = 1.0 / (ROPE_BASE ** (jnp.arange(nf, dtype=F32) / nf))
        half = d // (width // 2)
        sub = (d % (width // 2)) // nf
        f = d % nf
        ang = jnp.where(half[None, :] == 0, pos[0][:, None], pos[1][:, None]) * inv[f][None, :]
        cos = jnp.where(active[None, :], jnp.cos(ang), 1.0)
        sin = jnp.where(active[None, :], jnp.where(sub[None, :] == 0, -1.0, 1.0) * jnp.sin(ang), 0.0)
        ident = jnp.ones((lc, LANE), F32), jnp.zeros((lc, LANE), F32)
        return jnp.concatenate([ident[0], cos]), jnp.concatenate([ident[1], sin])

    scos, ssin = table(lane % HEAD_DIM, HEAD_DIM, jnp.ones((LANE,), bool))
    m_act = (lane >= MLA_NOPE) & (lane < MLA_NOPE + MLA_ROPE)
    mcos, msin = table(jnp.clip(lane - MLA_NOPE, 0, MLA_ROPE - 1), MLA_ROPE, m_act)
    return scos, ssin, mcos, msin


def _extend_w_in(w_in):
    depth, d, _ = w_in.shape
    z = lambda n: jnp.zeros((depth, d, n), w_in.dtype)
    o_swa, o_mla, o_ssd = NA_IN, NA_IN + SWA_IN, NA_IN + SWA_IN + MLA_IN
    cols = [w_in[..., 0:NA_IN]]
    rep = SWA_HEADS // SWA_KV_HEADS
    for hq in range(SWA_HEADS):
        qh = w_in[..., o_swa + hq * HEAD_DIM:o_swa + (hq + 1) * HEAD_DIM]
        cols += [qh, z(HEAD_DIM)] if hq // rep == 0 else [z(HEAD_DIM), qh]
    cols.append(w_in[..., o_swa + SWA_HEADS * HEAD_DIM:o_mla])
    cols.append(w_in[..., o_mla:o_mla + MLA_Q_LORA + MLA_KV_LORA])
    kr = w_in[..., o_mla + MLA_Q_LORA + MLA_KV_LORA:o_ssd]
    for _ in range(MLA_HEADS):
        cols += [z(MLA_NOPE), kr, z(LANE - MLA_NOPE - MLA_ROPE)]
    cols.append(w_in[..., o_ssd:o_ssd + SSD_INNER + SSD_CONV_CH])
    cols += [w_in[..., o_ssd + SSD_INNER + SSD_CONV_CH:], z(LANE - 2 * SSD_HEADS)]
    w = jnp.concatenate(cols, axis=-1).astype(BF16)
    assert w.shape[-1] == N_EXT
    return w


def _mla_weights(w_uq, w_ukv):
    depth = w_uq.shape[0]
    dq = MLA_NOPE + MLA_ROPE
    uq, uk, uv = [], [], []
    for h in range(MLA_HEADS):
        uq += [w_uq[..., h * dq:(h + 1) * dq], jnp.zeros((depth, MLA_Q_LORA, LANE - dq), w_uq.dtype)]
        uk += [w_ukv[..., h * LANE:h * LANE + MLA_NOPE], jnp.zeros((depth, MLA_KV_LORA, LANE - MLA_NOPE), w_ukv.dtype)]
        uv.append(w_ukv[..., h * LANE + MLA_NOPE:(h + 1) * LANE])
    cat = lambda xs: jnp.concatenate(xs, axis=-1).astype(BF16)
    return cat(uq), cat(uk), cat(uv)


def _permute_w_out(w_out):
    blocks = [w_out[:, :GROUP_W]]
    for h in (0, 2, 1, 3):
        blocks.append(w_out[:, GROUP_W + h * HEAD_DIM:GROUP_W + (h + 1) * HEAD_DIM])
    blocks.append(w_out[:, 2 * GROUP_W:])
    return jnp.concatenate(blocks, axis=1).astype(BF16)


def kernel(x, c, ctx, c_ctx, w_mod, b_mod, g_norm1, w_in, na_rpb, swa_sink, mla_g_q, mla_g_kv, mla_w_uq, mla_w_ukv,
           ssd_conv_w, ssd_conv_b, ssd_dt_bias, ssd_a_log, ssd_d, ssd_g_norm, w_out, g_norm2, w_mlp1, w_mlp2, g_final):
    bsz, s, d = x.shape
    lc = ctx.shape[1]
    depth = w_in.shape[0]
    n_tok = lc + s
    rows = s // GRID_W
    tiles_per_batch = n_tok // TOK_TILE
    assert d == D_MODEL and lc == TOK_TILE and s % (NA_QROWS * GRID_W) == 0 and rows >= NA_KROWS + NA_QROWS
    assert bsz + 1 <= MOD_ROWS

    cvec = jnp.concatenate([c_ctx[None], c, jnp.zeros((MOD_ROWS - 1 - bsz, d), F32)], axis=0)
    mods = _modulation(cvec, w_mod, b_mod).reshape(depth, MOD_ROWS, 6, d)

    tabs = _rope_tables(s, lc)
    w_ext = _extend_w_in(w_in)
    wuq, wk, wv = _mla_weights(mla_w_uq, mla_w_ukv)
    wo = _permute_w_out(w_out)
    w1 = w_mlp1.astype(BF16)
    w2 = w_mlp2.astype(BF16)
    pad_lanes = lambda a: jnp.pad(a.reshape(depth, 1, -1), ((0, 0), (0, 0), (0, LANE - a.shape[-1] * a.shape[-2])))
    dt_bias = pad_lanes(ssd_dt_bias)
    a_log = pad_lanes(ssd_a_log)
    conv_w = jnp.pad(ssd_conv_w, ((0, 0), (0, SSD_HALO - SSD_CONV), (0, 0)))
    skip = jnp.repeat(ssd_d, HEAD_DIM, axis=-1)

    h = jnp.concatenate([ctx, x], axis=1).reshape(bsz * n_tok, d)
    for l in range(depth):
        row = lambda a: a[l].reshape(1, -1)
        pa, pb, qm, km, vm, pz, pxbc, dtp = _inproj(
            h, mods[l], row(g_norm1), w_ext[l], tabs, row(mla_g_q), row(mla_g_kv), wuq[l], wk[l], wv[l],
            tiles_per_batch)
        per_batch = lambda a: a.reshape(bsz, n_tok, a.shape[-1])
        oa = _na_attention(per_batch(pa), _na_bias_table(na_rpb[l], rows), rows, lc)
        ob = _swa_attention(per_batch(pb), swa_sink[l], lc)
        om = _mla_attention(per_batch(qm), per_batch(km), per_batch(vm), lc)
        ssd_args = (per_batch(pz), per_batch(pxbc), per_batch(dtp), conv_w[l], row(ssd_conv_b), dt_bias[l], a_log[l], lc)
        yf = _ssd_scan(*ssd_args, reverse=False)
        od = _ssd_scan(*ssd_args, reverse=True, yf=yf, skip=row(skip), gnorm=row(ssd_g_norm))
        flat = lambda a: a.reshape(bsz * n_tok, a.shape[-1])
        h = _outmlp(h, flat(oa), flat(ob), flat(om), flat(od), mods[l], wo[l], row(g_norm2), w1[l], w2[l],
                    tiles_per_batch)
    return _final_norm(h.reshape(bsz, n_tok, d), g_final.reshape(1, d), lc)
```

```python
import functools
import math

import jax
import numpy as np
import jax.numpy as jnp
from jax import lax
from jax.experimental import pallas as pl
from jax.experimental.pallas import tpu as pltpu

F32 = jnp.float32
BF16 = jnp.bfloat16

D_MODEL = 1024
GRID_W = 64
HEAD_DIM = 64
D_FF = 4 * D_MODEL
NORM_EPS = 1e-6
ROPE_BASE = 10000.0
NEG_INF = -1e30
GROUP_W = D_MODEL // 4

NA_HEADS = 4
WIN_H = 8
WIN_W = 16
NA_QROWS = 4
NA_KROWS = 12

SWA_HEADS = 4
SWA_KV_HEADS = 2
SWA_WINDOW = 128
SWA_BLOCK = 128

MLA_HEADS = 4
MLA_Q_LORA = 256
MLA_KV_LORA = 128
MLA_NOPE = 64
MLA_ROPE = 32
MLA_V = 64
MLA_Q_SCALE = (MLA_NOPE + MLA_ROPE) ** -0.5 * math.log2(math.e)

SSD_INNER = GROUP_W
SSD_HEADS = 4
SSD_GROUPS = 2
SSD_STATE = 128
SSD_CONV = 5
SSD_CHUNK = 128
SSD_CONV_CH = SSD_INNER + 2 * SSD_GROUPS * SSD_STATE
SSD_HALO = 8

NA_IN = 3 * NA_HEADS * HEAD_DIM
SWA_IN = (SWA_HEADS + 2 * SWA_KV_HEADS) * HEAD_DIM
MLA_IN = MLA_Q_LORA + MLA_KV_LORA + MLA_ROPE
SSD_IN = 2 * SSD_INNER + 2 * SSD_GROUPS * SSD_STATE + 2 * SSD_HEADS

LANE = 128
TOK_TILE = 256
FF_TILE = 512
MOD_ROWS = 8
VMEM_LIMIT = 56 * 1024 * 1024

C_PA = 0
C_PB = C_PA + NA_IN
PB_W = SWA_HEADS * LANE + 2 * SWA_KV_HEADS * HEAD_DIM
C_CQ = C_PB + PB_W
C_CKV = C_CQ + MLA_Q_LORA
C_KR = C_CKV + MLA_KV_LORA
C_Z = C_KR + MLA_HEADS * LANE
C_XBC = C_Z + SSD_INNER
C_DT = C_XBC + SSD_CONV_CH
N_EXT = C_DT + LANE


def _dot(a, b):
    return jnp.dot(a, b, preferred_element_type=F32)


def _dot_nt(a, b):
    return lax.dot_general(a, b, (((1,), (1,)), ((), ())), preferred_element_type=F32)


def _dot_exact(a, b):
    return jnp.dot(a, b, preferred_element_type=F32, precision=lax.Precision.HIGHEST)


def _rms(x, g):
    return x * lax.rsqrt(jnp.mean(x * x, axis=-1, keepdims=True) + NORM_EPS) * g


def _silu(x):
    return x * jax.nn.sigmoid(x)


def _rope(x, cos, sin, half):
    n = x.shape[0]
    lane = lax.broadcasted_iota(jnp.int32, (n, LANE), 1)
    first = (lane % (2 * half)) < half
    outs = []
    for c in range(x.shape[1] // LANE):
        xc = x[:, c * LANE:(c + 1) * LANE]
        rot = jnp.where(first, pltpu.roll(xc, LANE - half, 1), pltpu.roll(xc, half, 1))
        outs.append(xc * cos + rot * sin)
    return outs[0] if len(outs) == 1 else jnp.concatenate(outs, axis=1)


def _params(n_axes):
    return pltpu.CompilerParams(dimension_semantics=("arbitrary",) * n_axes, vmem_limit_bytes=VMEM_LIMIT)


def _mod_kernel(c_ref, w_ref, b_ref, o_ref):
    o_ref[...] = _dot_exact(_silu(c_ref[...]), w_ref[...]) + b_ref[...]


def _modulation(cvec, w_mod, b_mod):
    depth, d, n6 = w_mod.shape
    tn = 1536
    return pl.pallas_call(
        _mod_kernel,
        grid=(depth, n6 // tn),
        in_specs=[
            pl.BlockSpec((MOD_ROWS, d), lambda l, j: (0, 0)),
            pl.BlockSpec((None, d, tn), lambda l, j: (l, 0, j)),
            pl.BlockSpec((None, 1, tn), lambda l, j: (l, 0, j)),
        ],
        out_specs=pl.BlockSpec((None, MOD_ROWS, tn), lambda l, j: (l, 0, j)),
        out_shape=jax.ShapeDtypeStruct((depth, MOD_ROWS, n6), F32),
        compiler_params=_params(2),
        name="modulation",
    )(cvec, w_mod, b_mod.reshape(depth, 1, n6))


def _mod_row(t, tiles_per_batch):
    return jnp.where(t % tiles_per_batch == 0, 0, 1 + t // tiles_per_batch)


def _inproj_kernel(h_ref, mod_ref, g1_ref, w_ref, scos_ref, ssin_ref, mcos_ref, msin_ref,
                   gq_ref, gkv_ref, wuq_ref, wk_ref, wv_ref,
                   pa_ref, pb_ref, qm_ref, km_ref, vm_ref, pz_ref, pxbc_ref, dt_ref):
    xn = _rms(h_ref[...], g1_ref[...])
    xm = (xn * (1.0 + mod_ref[1:2, :]) + mod_ref[0:1, :]).astype(BF16)

    def proj(lo, hi):
        return _dot(xm, w_ref[:, lo:hi])

    pa_ref[...] = proj(C_PA, C_PB).astype(BF16)

    n_qk = (SWA_HEADS + 1) * LANE
    qk = _rope(proj(C_PB, C_PB + n_qk), scos_ref[...], ssin_ref[...], HEAD_DIM // 4)
    pb_ref[:, 0:n_qk] = qk.astype(BF16)
    pb_ref[:, n_qk:PB_W] = proj(C_PB + n_qk, C_CQ).astype(BF16)

    mcos = mcos_ref[...]
    msin = msin_ref[...]
    cq = _rms(proj(C_CQ, C_CKV), gq_ref[...]).astype(BF16)
    qm_ref[...] = (_rope(_dot(cq, wuq_ref[...]), mcos, msin, MLA_ROPE // 4) * MLA_Q_SCALE).astype(BF16)
    ckv = _rms(proj(C_CKV, C_KR), gkv_ref[...]).astype(BF16)
    kr = _rope(proj(C_KR, C_Z), mcos, msin, MLA_ROPE // 4)
    km_ref[...] = (_dot(ckv, wk_ref[...]) + kr).astype(BF16)
    vm_ref[...] = _dot(ckv, wv_ref[...]).T.astype(BF16)

    pz_ref[...] = proj(C_Z, C_XBC)
    pxbc_ref[...] = proj(C_XBC, C_DT)
    dt_ref[...] = proj(C_DT, N_EXT)


def _inproj(h, mods, g1, w_ext, tabs, gq, gkv, wuq, wk, wv, tiles_per_batch):
    n_tok = h.shape[0]
    tm = TOK_TILE
    scos, ssin, mcos, msin = tabs
    tok = lambda n: pl.BlockSpec((tm, n), lambda t: (t, 0))
    full = lambda a: pl.BlockSpec(a.shape, lambda t: (0,) * a.ndim)
    tab = pl.BlockSpec((tm, LANE), lambda t: (t % tiles_per_batch, 0))
    widths = (NA_IN, PB_W, MLA_HEADS * LANE, MLA_HEADS * LANE, None, SSD_INNER, SSD_CONV_CH, LANE)
    dtypes = (BF16, BF16, BF16, BF16, BF16, F32, F32, F32)
    tk = _mla_chunk(tiles_per_batch * tm)
    sub = tk // tm
    wv_out = MLA_HEADS * MLA_V
    vt_spec = pl.BlockSpec((None, wv_out, tm), lambda t: (t // sub, 0, t % sub))
    vt_shape = jax.ShapeDtypeStruct((n_tok // tk, wv_out, tk), BF16)
    return pl.pallas_call(
        _inproj_kernel,
        grid=(n_tok // tm,),
        in_specs=[
            tok(D_MODEL),
            pl.BlockSpec((None, 6, D_MODEL), lambda t: (_mod_row(t, tiles_per_batch), 0, 0)),
            full(g1), full(w_ext), tab, tab, tab, tab, full(gq), full(gkv), full(wuq), full(wk), full(wv),
        ],
        out_specs=[vt_spec if n is None else tok(n) for n in widths],
        out_shape=[vt_shape if n is None else jax.ShapeDtypeStruct((n_tok, n), dt) for n, dt in zip(widths, dtypes)],
        compiler_params=_params(1),
        name="inproj",
    )(h, mods, g1, w_ext, scos, ssin, mcos, msin, gq, gkv, wuq, wk, wv)


def _na_kernel(q_ref, k_ref, v_ref, tab_ref, o_ref, *, rows, lc):
    i = pl.program_id(1)
    scale = HEAD_DIM ** -0.5
    nq = q_ref.shape[0]
    q = q_ref[...]
    lane = lax.broadcasted_iota(jnp.int32, (nq, NA_HEADS * HEAD_DIM), 1)
    k_ctx = k_ref[0:lc, :]
    v_ctx = v_ref[0:lc, :]

    def attend(k_win, v_win):
        out = jnp.zeros((nq, NA_HEADS * HEAD_DIM), F32)
        for h in range(NA_HEADS):
            in_head = (lane // HEAD_DIM) == h
            qh = jnp.where(in_head, q, jnp.zeros_like(q))
            s_c = _dot_nt(qh, k_ctx) * scale
            m = jnp.max(s_c, axis=-1, keepdims=True)
            if k_win is not None:
                s_w = _dot_nt(qh, k_win) * scale + tab_ref[h]
                m = jnp.maximum(m, jnp.max(s_w, axis=-1, keepdims=True))
                p_w = jnp.exp(s_w - m)
            p_c = jnp.exp(s_c - m)
            den = jnp.sum(p_c, axis=-1, keepdims=True)
            o = _dot(p_c.astype(BF16), v_ctx)
            if k_win is not None:
                den = den + jnp.sum(p_w, axis=-1, keepdims=True)
                o = o + _dot(p_w.astype(BF16), v_win)
            out = jnp.where(in_head, o / den, out)
        o_ref[...] = out.astype(o_ref.dtype)

    @pl.when(i == 0)
    def _():
        attend(None, None)

    @pl.when(i > 0)
    def _():
        r0 = (i - 1) * NA_QROWS
        ws = jnp.clip(r0 - WIN_H // 2, 0, rows - NA_KROWS)
        start = pl.multiple_of(lc + ws * GRID_W, GRID_W)
        nk = NA_KROWS * GRID_W
        attend(k_ref[pl.ds(start, nk), :], v_ref[pl.ds(start, nk), :])


def _na_attention(pa, table, rows, lc):
    bsz, n_tok, _ = pa.shape
    nq = NA_QROWS * GRID_W
    groups = rows // NA_QROWS
    w = NA_HEADS * HEAD_DIM

    def cfg(b, i):
        g = i - 1
        return (jnp.where(g <= 0, 0, jnp.where(g == groups - 1, 2, 1)), 0, 0, 0)

    return pl.pallas_call(
        functools.partial(_na_kernel, rows=rows, lc=lc),
        grid=(bsz, 1 + groups),
        in_specs=[
            pl.BlockSpec((None, nq, w), lambda b, i: (b, i, 0)),
            pl.BlockSpec((None, n_tok, w), lambda b, i: (b, 0, 1)),
            pl.BlockSpec((None, n_tok, w), lambda b, i: (b, 0, 2)),
            pl.BlockSpec((None, NA_HEADS, nq, NA_KROWS * GRID_W), cfg),
        ],
        out_specs=pl.BlockSpec((None, nq, w), lambda b, i: (b, i, 0)),
        out_shape=jax.ShapeDtypeStruct((bsz, n_tok, w), BF16),
        compiler_params=_params(2),
        name="na_attention",
    )(pa, pa, pa, table)


def _na_bias_tables(rpb, rows):
    groups = rows // NA_QROWS
    i = np.arange(NA_QROWS)
    j = np.arange(NA_KROWS)
    col = np.arange(GRID_W)
    cstart = np.clip(col - WIN_W // 2, 0, GRID_W - WIN_W)
    ok_c = (col[None, :] >= cstart[:, None]) & (col[None, :] < cstart[:, None] + WIN_W)
    dc = np.clip(col[None, :] - col[:, None] + (WIN_W - 1), 0, 2 * WIN_W - 2)
    sel_c = (dc[:, None, :] == np.arange(2 * WIN_W - 1)[None, :, None]).astype(np.float32)
    sel_r, ok = [], []
    for g in (0, 1, groups - 1):
        r = g * NA_QROWS + i
        kr = np.clip(g * NA_QROWS - WIN_H // 2, 0, rows - NA_KROWS) + j
        rstart = np.clip(r - WIN_H // 2, 0, rows - WIN_H)
        ok_r = (kr[None, :] >= rstart[:, None]) & (kr[None, :] < rstart[:, None] + WIN_H)
        dr = np.clip(kr[None, :] - r[:, None] + (WIN_H - 1), 0, 2 * WIN_H - 2)
        sel_r.append((dr[:, None, :] == np.arange(2 * WIN_H - 1)[None, :, None]).astype(np.float32))
        ok.append(ok_r[:, None, :, None] & ok_c[None, :, None, :])
    hi = lax.Precision.HIGHEST
    t = jnp.einsum("ciaj,lhab->lchijb", np.stack(sel_r), rpb, precision=hi)
    t = jnp.einsum("lchijb,qbk->lchiqjk", t, sel_c, precision=hi)
    t = jnp.where(np.stack(ok)[None, :, None], t, NEG_INF)
    return t.reshape(rpb.shape[0], 3, NA_HEADS, NA_QROWS * GRID_W, NA_KROWS * GRID_W)


def _swa_kernel(sink_ref, q_ref, k_ref, v_ref, o_ref, *, nblk, lc):
    i = pl.program_id(1)
    scale = HEAD_DIM ** -0.5
    blk = SWA_BLOCK
    rep = SWA_HEADS // SWA_KV_HEADS
    k_ctx = k_ref[0:lc, :]
    v_ctx = v_ref[0:lc, :]
    row = lax.broadcasted_iota(jnp.int32, (rep * blk, 1), 0)

    def attend(k_win, v_win, valid):
        outs = []
        for g in range(SWA_KV_HEADS):
            qq = jnp.concatenate([q_ref[:, (rep * g + r) * LANE:(rep * g + r + 1) * LANE] for r in range(rep)], axis=0)
            sink = jnp.where(row < blk, sink_ref[rep * g], sink_ref[rep * g + 1])
            s_c = _dot_nt(qq, k_ctx) * scale
            m = jnp.maximum(jnp.max(s_c, axis=-1, keepdims=True), sink)
            if k_win is not None:
                s_w = jnp.where(valid, _dot_nt(qq, k_win) * scale, NEG_INF)
                m = jnp.maximum(m, jnp.max(s_w, axis=-1, keepdims=True))
                p_w = jnp.exp(s_w - m)
            p_c = jnp.exp(s_c - m)
            den = jnp.sum(p_c, axis=-1, keepdims=True) + jnp.exp(sink - m)
            o = _dot(p_c.astype(BF16), v_ctx)
            if k_win is not None:
                den = den + jnp.sum(p_w, axis=-1, keepdims=True)
                o = o + _dot(p_w.astype(BF16), v_win)
            outs.append(o / den)
        lane = lax.broadcasted_iota(jnp.int32, (blk, LANE), 1)
        lo = lane < HEAD_DIM
        o_ref[:, 0:LANE] = jnp.where(lo, outs[0][0:blk], outs[1][0:blk]).astype(o_ref.dtype)
        o_ref[:, LANE:2 * LANE] = jnp.where(lo, outs[0][blk:2 * blk], outs[1][blk:2 * blk]).astype(o_ref.dtype)

    n_ctx_blk = lc // blk

    @pl.when(i < n_ctx_blk)
    def _():
        attend(None, None, None)

    @pl.when(i >= n_ctx_blk)
    def _():
        n = i - n_ctx_blk
        wb = jnp.clip(n - 1, 0, nblk - 3)
        start = pl.multiple_of(lc + wb * blk, blk)
        iq = lax.broadcasted_iota(jnp.int32, (rep * blk, 3 * blk), 0) % blk
        ik = lax.broadcasted_iota(jnp.int32, (rep * blk, 3 * blk), 1)
        dist = (n - wb) * blk + iq - ik
        valid = jnp.abs(dist) <= SWA_WINDOW
        attend(k_ref[pl.ds(start, 3 * blk), :], v_ref[pl.ds(start, 3 * blk), :], valid)


def _swa_attention(pb, sink, lc):
    bsz, n_tok, _ = pb.shape
    blk = SWA_BLOCK
    nblk = (n_tok - lc) // blk
    qw = SWA_HEADS * LANE
    kw = SWA_KV_HEADS * HEAD_DIM
    return pl.pallas_call(
        functools.partial(_swa_kernel, nblk=nblk, lc=lc),
        grid=(bsz, n_tok // blk),
        in_specs=[
            pl.BlockSpec(memory_space=pltpu.SMEM),
            pl.BlockSpec((None, blk, qw), lambda b, i: (b, i, 0)),
            pl.BlockSpec((None, n_tok, kw), lambda b, i: (b, 0, qw // kw)),
            pl.BlockSpec((None, n_tok, kw), lambda b, i: (b, 0, qw // kw + 1)),
        ],
        out_specs=pl.BlockSpec((None, blk, SWA_HEADS * HEAD_DIM), lambda b, i: (b, i, 0)),
        out_shape=jax.ShapeDtypeStruct((bsz, n_tok, SWA_HEADS * HEAD_DIM), BF16),
        compiler_params=_params(2),
        name="swa_attention",
    )(sink, pb, pb, pb)


def _mla_chunk(n_tok):
    return next(c for c in (3 * TOK_TILE, 2 * TOK_TILE, TOK_TILE) if n_tok % c == 0)


def _reduce_rows(x, fn):
    n, w = x.shape
    for k in (8, 4, 2):
        while n > 8 and n % (8 * k) == 0:
            x = fn(x.reshape(k, n // k, w), axis=0)
            n //= k
    return fn(x, axis=0, keepdims=True)


def _mla_kernel(q_ref, k_ref, vt_ref, o_ref, qbd_ref, sa_ref, sb_ref, *, lc):
    i = pl.program_id(1)
    n_chunks, _, tk = vt_ref.shape
    tq = q_ref.shape[0]
    pairs = MLA_HEADS // 2

    lane = lax.broadcasted_iota(jnp.int32, (tq, 2 * LANE), 1)
    for j in range(pairs):
        qp = q_ref[:, 2 * j * LANE:2 * (j + 1) * LANE]
        qbd_ref[j, 0:tq, :] = jnp.where(lane < LANE, qp, jnp.zeros_like(qp))
        qbd_ref[j, tq:2 * tq, :] = jnp.where(lane >= LANE, qp, jnp.zeros_like(qp))

    def attend(n_trips, width):
        assert n_trips % 2 == 1

        def scores(c, buf):
            st = pl.multiple_of(c * tk, tk)
            maxima = []
            for j in range(pairs):
                s = _dot_nt(k_ref[pl.ds(st, width), 2 * j * LANE:2 * (j + 1) * LANE], qbd_ref[j])
                buf[j, 0:width, :] = s
                maxima.append(_reduce_rows(s, jnp.max))
            return tuple(maxima)

        def update(c, buf, maxima, state):
            new = []
            for j, (m_old, den, acc_a, acc_b) in enumerate(state):
                m_new = jnp.maximum(m_old, maxima[j])
                alpha = jnp.exp2(m_old - m_new)
                p = jnp.exp2(buf[j, 0:width, :] - m_new)
                den = alpha * den + _reduce_rows(p, jnp.sum)
                pv = _dot(vt_ref[c, 2 * j * MLA_V:2 * (j + 1) * MLA_V, 0:width], p.astype(BF16))
                acc_a = alpha[:, 0:tq] * acc_a + pv[0:MLA_V, 0:tq]
                acc_b = alpha[:, tq:2 * tq] * acc_b + pv[MLA_V:2 * MLA_V, tq:2 * tq]
                new.append((m_new, den, acc_a, acc_b))
            return tuple(new)

        def body(t, carry):
            maxima, state = carry
            nxt = scores(2 * t + 1, sb_ref)
            state = update(2 * t, sa_ref, maxima, state)
            maxima = scores(2 * t + 2, sa_ref)
            state = update(2 * t + 1, sb_ref, nxt, state)
            return maxima, state

        init = (jnp.full((1, 2 * tq), NEG_INF, F32), jnp.zeros((1, 2 * tq), F32),
                jnp.zeros((MLA_V, tq), F32), jnp.zeros((MLA_V, tq), F32))
        maxima, state = lax.fori_loop(0, n_trips // 2, body, (scores(0, sa_ref), (init,) * pairs))
        state = update(n_trips - 1, sa_ref, maxima, state)
        out = []
        for _, den, acc_a, acc_b in state:
            out += [acc_a / den[:, 0:tq], acc_b / den[:, tq:2 * tq]]
        o_ref[...] = jnp.concatenate(out, axis=0).T.astype(o_ref.dtype)

    @pl.when(i == 0)
    def _():
        attend(1, lc)

    @pl.when(i > 0)
    def _():
        attend(n_chunks, tk)


def _mla_attention(qm, km, vt, lc):
    bsz, n_tok, qw = qm.shape
    tq = TOK_TILE
    assert lc == tq
    wv = MLA_HEADS * MLA_V
    tk = vt.shape[-1]
    return pl.pallas_call(
        functools.partial(_mla_kernel, lc=lc),
        grid=(bsz, n_tok // tq),
        in_specs=[
            pl.BlockSpec((None, tq, qw), lambda b, i: (b, i, 0)),
            pl.BlockSpec((None, n_tok, qw), lambda b, i: (b, 0, 0)),
            pl.BlockSpec((None, n_tok // tk, wv, tk), lambda b, i: (b, 0, 0, 0)),
        ],
        out_specs=pl.BlockSpec((None, tq, wv), lambda b, i: (b, i, 0)),
        out_shape=jax.ShapeDtypeStruct((bsz, n_tok, wv), BF16),
        scratch_shapes=[pltpu.VMEM((MLA_HEADS // 2, 2 * tq, 2 * LANE), BF16),
                        pltpu.VMEM((MLA_HEADS // 2, tk, 2 * tq), F32), pltpu.VMEM((MLA_HEADS // 2, tk, 2 * tq), F32)],
        compiler_params=_params(2),
        name="mla_attention",
    )(qm, km, vt.reshape(bsz, n_tok // tk, wv, tk))


def _ssd_chunk(step, reverse, n_ctx, n_chunks):
    if not reverse:
        return step
    return jnp.where(step < n_ctx, n_ctx - 1 - step, n_chunks + n_ctx - 1 - step)


def _expand_heads(v, base):
    q = v.shape[0]
    lane = lax.broadcasted_iota(jnp.int32, (q, SSD_INNER), 1)
    out = jnp.broadcast_to(v[:, base:base + 1], (q, SSD_INNER))
    for h in range(1, SSD_HEADS):
        out = jnp.where(lane >= h * HEAD_DIM, jnp.broadcast_to(v[:, base + h:base + h + 1], (q, SSD_INNER)), out)
    return out


def _ssd_kernel(*refs, reverse, n_ctx, n_chunks):
    if reverse:
        (z_ref, xbc_ref, prev_ref, next_ref, dt_ref, cw_ref, cb_ref, dtb_ref, alog_ref,
         yf_ref, skip_ref, gn_ref, o_ref, ext_ref, hs_ref) = refs
    else:
        (xbc_ref, prev_ref, next_ref, dt_ref, cw_ref, cb_ref, dtb_ref, alog_ref, o_ref, ext_ref, hs_ref) = refs
    step = pl.program_id(1)
    chunk = _ssd_chunk(step, reverse, n_ctx, n_chunks)
    q = SSD_CHUNK
    pad = SSD_CONV // 2
    base = SSD_HEADS if reverse else 0
    last = 0 if reverse else q - 1

    @pl.when(step == 0)
    def _():
        hs_ref[...] = jnp.zeros_like(hs_ref)

    has_prev = jnp.logical_and(chunk != 0, chunk != n_ctx)
    has_next = jnp.logical_and(chunk != n_ctx - 1, chunk != n_chunks - 1)
    ext_ref[0:SSD_HALO, :] = jnp.where(has_prev, prev_ref[...], 0.0)
    ext_ref[SSD_HALO:SSD_HALO + q, :] = xbc_ref[...]
    ext_ref[SSD_HALO + q:, :] = jnp.where(has_next, next_ref[...], 0.0)
    acc = ext_ref[pl.ds(SSD_HALO - pad, q), :] * cw_ref[0:1, :]
    for t in range(1, SSD_CONV):
        acc = acc + ext_ref[pl.ds(SSD_HALO - pad + t, q), :] * cw_ref[t:t + 1, :]
    xbc = _silu(acc + cb_ref[...])
    x = xbc[:, 0:SSD_INNER]

    dt_in = dt_ref[...] + dtb_ref[...]
    dt = jnp.maximum(dt_in, 0.0) + jnp.log1p(jnp.exp(-jnp.abs(dt_in)))
    da = dt * (-jnp.exp(alog_ref[...]))
    ii = lax.broadcasted_iota(jnp.int32, (q, q), 0)
    jj = lax.broadcasted_iota(jnp.int32, (q, q), 1)
    lower = (jj <= ii).astype(F32)
    upper = (jj >= ii).astype(F32)
    tri, tri_t = (upper, lower) if reverse else (lower, upper)
    cum = _dot_exact(tri, da)
    cum_t = _dot_exact(da.T, tri_t)
    causal = (jj >= ii) if reverse else (jj <= ii)

    dtx = _expand_heads(dt, base)
    cum_x = _expand_heads(cum, base)
    xdt = x * dtx
    xdt_b = xdt.astype(BF16)
    cum_last = cum_x[last:last + 1, :]
    xdtd_b = (xdt * jnp.exp(cum_last - cum_x)).astype(BF16)
    lane = lax.broadcasted_iota(jnp.int32, (q, SSD_INNER), 1)
    hs = hs_ref[...]
    hs_b = hs.astype(BF16)

    y = jnp.zeros((q, SSD_INNER), F32)
    y_off = []
    new_state = []
    for g in range(SSD_GROUPS):
        bm = xbc[:, SSD_INNER + g * SSD_STATE:SSD_INNER + (g + 1) * SSD_STATE]
        cm = xbc[:, SSD_INNER + (SSD_GROUPS + g) * SSD_STATE:SSD_INNER + (SSD_GROUPS + g + 1) * SSD_STATE]
        cm_b = cm.astype(BF16)
        cb = _dot_nt(cm_b, bm.astype(BF16))
        for h in range(g * SSD_HEADS // SSD_GROUPS, (g + 1) * SSD_HEADS // SSD_GROUPS):
            col = cum[:, base + h:base + h + 1]
            rowv = cum_t[base + h:base + h + 1, :]
            seg = jnp.exp(jnp.where(causal, col - rowv, -jnp.inf))
            yd = _dot((cb * seg).astype(BF16), xdt_b)
            y = jnp.where((lane // HEAD_DIM) == h, yd, y)
        y_off.append(_dot(cm_b, hs_b))
        new_state.append(_dot(bm.T.astype(BF16), xdtd_b))
    half = lane < SSD_INNER // SSD_GROUPS
    y = y + jnp.where(half, y_off[0], y_off[1]) * jnp.exp(cum_x)
    hs_ref[...] = hs * jnp.exp(cum_last) + jnp.where(half, new_state[0], new_state[1])

    if reverse:
        y = x * skip_ref[...] + yf_ref[...] + y
        o_ref[...] = _rms(y * _silu(z_ref[...]), gn_ref[...]).astype(o_ref.dtype)
    else:
        o_ref[...] = y


def _ssd_scan(pz, pxbc, dtp, conv_w, conv_b, dt_bias, a_log, lc, reverse, yf=None, skip=None, gnorm=None):
    bsz, n_tok, _ = pxbc.shape
    q = SSD_CHUNK
    n_chunks = n_tok // q
    n_ctx = lc // q
    per = q // SSD_HALO
    chunk = functools.partial(_ssd_chunk, reverse=reverse, n_ctx=n_ctx, n_chunks=n_chunks)
    tok = lambda n: pl.BlockSpec((None, q, n), lambda b, s: (b, chunk(s), 0))
    full = lambda a: pl.BlockSpec(a.shape, lambda b, s: (0,) * a.ndim)
    prev = pl.BlockSpec((None, SSD_HALO, SSD_CONV_CH), lambda b, s: (b, jnp.maximum(chunk(s) * per - 1, 0), 0))
    nxt = pl.BlockSpec((None, SSD_HALO, SSD_CONV_CH),
                       lambda b, s: (b, jnp.minimum((chunk(s) + 1) * per, n_tok // SSD_HALO - 1), 0))
    in_specs = [tok(SSD_CONV_CH), prev, nxt, tok(LANE), full(conv_w), full(conv_b), full(dt_bias), full(a_log)]
    args = [pxbc, pxbc, pxbc, dtp, conv_w, conv_b, dt_bias, a_log]
    if reverse:
        in_specs = [tok(SSD_INNER)] + in_specs + [tok(SSD_INNER), full(skip), full(gnorm)]
        args = [pz] + args + [yf, skip, gnorm]
    return pl.pallas_call(
        functools.partial(_ssd_kernel, reverse=reverse, n_ctx=n_ctx, n_chunks=n_chunks),
        grid=(bsz, n_chunks),
        in_specs=in_specs,
        out_specs=tok(SSD_INNER),
        out_shape=jax.ShapeDtypeStruct((bsz, n_tok, SSD_INNER), BF16 if reverse else F32),
        scratch_shapes=[pltpu.VMEM((q + 2 * SSD_HALO, SSD_CONV_CH), F32), pltpu.VMEM((SSD_STATE, SSD_INNER), F32)],
        compiler_params=_params(2),
        name="ssd_bwd" if reverse else "ssd_fwd",
    )(*args)


def _outmlp_kernel(h_ref, oa_ref, ob_ref, om_ref, od_ref, mod_ref, wo_ref, g2_ref, w1_ref, w2_ref, o_ref):
    mix = jnp.concatenate([oa_ref[...], ob_ref[...], om_ref[...], od_ref[...]], axis=1)
    h1 = h_ref[...] + mod_ref[2:3, :] * _dot(mix, wo_ref[...])
    xm = (_rms(h1, g2_ref[...]) * (1.0 + mod_ref[4:5, :]) + mod_ref[3:4, :]).astype(BF16)
    acc = jnp.zeros(h1.shape, F32)
    for c in range(D_FF // FF_TILE):
        a = jnp.maximum(_dot(xm, w1_ref[:, c * FF_TILE:(c + 1) * FF_TILE]), 0.0)
        acc = acc + _dot((a * a).astype(BF16), w2_ref[c * FF_TILE:(c + 1) * FF_TILE, :])
    o_ref[...] = h1 + mod_ref[5:6, :] * acc


def _outmlp(h, oa, ob, om, od, mods, wo, g2, w1, w2, tiles_per_batch):
    n_tok = h.shape[0]
    tm = TOK_TILE
    tok = lambda n: pl.BlockSpec((tm, n), lambda t: (t, 0))
    full = lambda a: pl.BlockSpec(a.shape, lambda t: (0,) * a.ndim)
    return pl.pallas_call(
        _outmlp_kernel,
        grid=(n_tok // tm,),
        in_specs=[
            tok(D_MODEL), tok(GROUP_W), tok(GROUP_W), tok(GROUP_W), tok(GROUP_W),
            pl.BlockSpec((None, 6, D_MODEL), lambda t: (_mod_row(t, tiles_per_batch), 0, 0)),
            full(wo), full(g2), full(w1), full(w2),
        ],
        out_specs=tok(D_MODEL),
        out_shape=jax.ShapeDtypeStruct((n_tok, D_MODEL), F32),
        compiler_params=_params(1),
        name="outproj_mlp",
    )(h, oa, ob, om, od, mods, wo, g2, w1, w2)


def _final_norm_kernel(h_ref, g_ref, o_ref):
    o_ref[...] = _rms(h_ref[...], g_ref[...])


def _final_norm(h, g, lc):
    bsz, n_tok, d = h.shape
    tm = TOK_TILE
    off = lc // tm
    return pl.pallas_call(
        _final_norm_kernel,
        grid=(bsz, (n_tok - lc) // tm),
        in_specs=[pl.BlockSpec((None, tm, d), lambda b, t: (b, t + off, 0)), pl.BlockSpec((1, d), lambda b, t: (0, 0))],
        out_specs=pl.BlockSpec((None, tm, d), lambda b, t: (b, t, 0)),
        out_shape=jax.ShapeDtypeStruct((bsz, n_tok - lc, d), F32),
        compiler_params=_params(2),
        name="final_norm",
    )(h, g)


def _rope_tables(s, lc):
    t = jnp.arange(s)
    pos = (t // GRID_W).astype(F32), (t % GRID_W).astype(F32)
    lane = jnp.arange(LANE)

    def table(d, width, active):
        nf = width // 4
        inv = 1.0 / (ROPE_BASE ** (jnp.arange(nf, dtype=F32) / nf))
        half = d // (width // 2)
        sub = (d % (width // 2)) // nf
        f = d % nf
        ang = jnp.where(half[None, :] == 0, pos[0][:, None], pos[1][:, None]) * inv[f][None, :]
        cos = jnp.where(active[None, :], jnp.cos(ang), 1.0)
        sin = jnp.where(active[None, :], jnp.where(sub[None, :] == 0, -1.0, 1.0) * jnp.sin(ang), 0.0)
        ident = jnp.ones((lc, LANE), F32), jnp.zeros((lc, LANE), F32)
        return jnp.concatenate([ident[0], cos]), jnp.concatenate([ident[1], sin])

    scos, ssin = table(lane % HEAD_DIM, HEAD_DIM, jnp.ones((LANE,), bool))
    m_act = (lane >= MLA_NOPE) & (lane < MLA_NOPE + MLA_ROPE)
    mcos, msin = table(jnp.clip(lane - MLA_NOPE, 0, MLA_ROPE - 1), MLA_ROPE, m_act)
    return scos, ssin, mcos, msin


def _extend_w_in(w_in):
    depth, d, _ = w_in.shape
    z = lambda n: jnp.zeros((depth, d, n), w_in.dtype)
    o_swa, o_mla, o_ssd = NA_IN, NA_IN + SWA_IN, NA_IN + SWA_IN + MLA_IN
    cols = [w_in[..., 0:NA_IN]]
    rep = SWA_HEADS // SWA_KV_HEADS
    for hq in range(SWA_HEADS):
        qh = w_in[..., o_swa + hq * HEAD_DIM:o_swa + (hq + 1) * HEAD_DIM]
        cols += [qh, z(HEAD_DIM)] if hq // rep == 0 else [z(HEAD_DIM), qh]
    cols.append(w_in[..., o_swa + SWA_HEADS * HEAD_DIM:o_mla])
    cols.append(w_in[..., o_mla:o_mla + MLA_Q_LORA + MLA_KV_LORA])
    kr = w_in[..., o_mla + MLA_Q_LORA + MLA_KV_LORA:o_ssd]
    for _ in range(MLA_HEADS):
        cols += [z(MLA_NOPE), kr, z(LANE - MLA_NOPE - MLA_ROPE)]
    cols.append(w_in[..., o_ssd:o_ssd + SSD_INNER + SSD_CONV_CH])
    cols += [w_in[..., o_ssd + SSD_INNER + SSD_CONV_CH:], z(LANE - 2 * SSD_HEADS)]
    w = jnp.concatenate(cols, axis=-1).astype(BF16)
    assert w.shape[-1] == N_EXT
    return w


def _mla_weights(w_uq, w_ukv):
    depth = w_uq.shape[0]
    dq = MLA_NOPE + MLA_ROPE
    uq, uk, uv = [], [], []
    for h in range(MLA_HEADS):
        uq += [w_uq[..., h * dq:(h + 1) * dq], jnp.zeros((depth, MLA_Q_LORA, LANE - dq), w_uq.dtype)]
        uk += [w_ukv[..., h * LANE:h * LANE + MLA_NOPE], jnp.zeros((depth, MLA_KV_LORA, LANE - MLA_NOPE), w_ukv.dtype)]
        uv.append(w_ukv[..., h * LANE + MLA_NOPE:(h + 1) * LANE])
    cat = lambda xs: jnp.concatenate(xs, axis=-1).astype(BF16)
    return cat(uq), cat(uk), cat(uv)


def _permute_w_out(w_out):
    blocks = [w_out[:, :GROUP_W]]
    for h in (0, 2, 1, 3):
        blocks.append(w_out[:, GROUP_W + h * HEAD_DIM:GROUP_W + (h + 1) * HEAD_DIM])
    blocks.append(w_out[:, 2 * GROUP_W:])
    return jnp.concatenate(blocks, axis=1).astype(BF16)


def kernel(x, c, ctx, c_ctx, w_mod, b_mod, g_norm1, w_in, na_rpb, swa_sink, mla_g_q, mla_g_kv, mla_w_uq, mla_w_ukv,
           ssd_conv_w, ssd_conv_b, ssd_dt_bias, ssd_a_log, ssd_d, ssd_g_norm, w_out, g_norm2, w_mlp1, w_mlp2, g_final):
    bsz, s, d = x.shape
    lc = ctx.shape[1]
    depth = w_in.shape[0]
    n_tok = lc + s
    rows = s // GRID_W
    tiles_per_batch = n_tok // TOK_TILE
    assert d == D_MODEL and lc == TOK_TILE and s % (NA_QROWS * GRID_W) == 0 and rows >= NA_KROWS + NA_QROWS
    assert bsz + 1 <= MOD_ROWS

    cvec = jnp.concatenate([c_ctx[None], c, jnp.zeros((MOD_ROWS - 1 - bsz, d), F32)], axis=0)
    mods = _modulation(cvec, w_mod, b_mod).reshape(depth, MOD_ROWS, 6, d)

    tabs = _rope_tables(s, lc)
    na_tables = _na_bias_tables(na_rpb, rows)
    w_ext = _extend_w_in(w_in)
    wuq, wk, wv = _mla_weights(mla_w_uq, mla_w_ukv)
    wo = _permute_w_out(w_out)
    w1 = w_mlp1.astype(BF16)
    w2 = w_mlp2.astype(BF16)
    pad_lanes = lambda a: jnp.pad(a.reshape(depth, 1, -1), ((0, 0), (0, 0), (0, LANE - a.shape[-1] * a.shape[-2])))
    dt_bias = pad_lanes(ssd_dt_bias)
    a_log = pad_lanes(ssd_a_log)
    conv_w = jnp.pad(ssd_conv_w, ((0, 0), (0, SSD_HALO - SSD_CONV), (0, 0)))
    skip = jnp.repeat(ssd_d, HEAD_DIM, axis=-1)

    h = jnp.concatenate([ctx, x], axis=1).reshape(bsz * n_tok, d)
    for l in range(depth):
        row = lambda a: a[l].reshape(1, -1)
        pa, pb, qm, km, vm, pz, pxbc, dtp = _inproj(
            h, mods[l], row(g_norm1), w_ext[l], tabs, row(mla_g_q), row(mla_g_kv), wuq[l], wk[l], wv[l],
            tiles_per_batch)
        per_batch = lambda a: a.reshape(bsz, n_tok, a.shape[-1])
        oa = _na_attention(per_batch(pa), na_tables[l], rows, lc)
        ob = _swa_attention(per_batch(pb), swa_sink[l], lc)
        om = _mla_attention(per_batch(qm), per_batch(km), vm, lc)
        ssd_args = (per_batch(pz), per_batch(pxbc), per_batch(dtp), conv_w[l], row(ssd_conv_b), dt_bias[l], a_log[l], lc)
        yf = _ssd_scan(*ssd_args, reverse=False)
        od = _ssd_scan(*ssd_args, reverse=True, yf=yf, skip=row(skip), gnorm=row(ssd_g_norm))
        flat = lambda a: a.reshape(bsz * n_tok, a.shape[-1])
        h = _outmlp(h, flat(oa), flat(ob), flat(om), flat(od), mods[l], wo[l], row(g_norm2), w1[l], w2[l],
                    tiles_per_batch)
    return _final_norm(h.reshape(bsz, n_tok, d), g_final.reshape(1, d), lc)
```

```python
import functools
import math

import jax
import numpy as np
import jax.numpy as jnp
from jax import lax
from jax.experimental import pallas as pl
from jax.experimental.pallas import tpu as pltpu

F32 = jnp.float32
BF16 = jnp.bfloat16

D_MODEL = 1024
GRID_W = 64
HEAD_DIM = 64
D_FF = 4 * D_MODEL
NORM_EPS = 1e-6
ROPE_BASE = 10000.0
NEG_INF = -1e30
GROUP_W = D_MODEL // 4

NA_HEADS = 4
WIN_H = 8
WIN_W = 16
NA_QROWS = 4
NA_KROWS = 12

SWA_HEADS = 4
SWA_KV_HEADS = 2
SWA_WINDOW = 128
SWA_BLOCK = 128

MLA_HEADS = 4
MLA_Q_LORA = 256
MLA_KV_LORA = 128
MLA_NOPE = 64
MLA_ROPE = 32
MLA_V = 64
MLA_Q_SCALE = (MLA_NOPE + MLA_ROPE) ** -0.5 * math.log2(math.e)

SSD_INNER = GROUP_W
SSD_HEADS = 4
SSD_GROUPS = 2
SSD_STATE = 128
SSD_CONV = 5
SSD_CHUNK = 128
SSD_CONV_CH = SSD_INNER + 2 * SSD_GROUPS * SSD_STATE
SSD_HALO = 8

NA_IN = 3 * NA_HEADS * HEAD_DIM
SWA_IN = (SWA_HEADS + 2 * SWA_KV_HEADS) * HEAD_DIM
MLA_IN = MLA_Q_LORA + MLA_KV_LORA + MLA_ROPE
SSD_IN = 2 * SSD_INNER + 2 * SSD_GROUPS * SSD_STATE + 2 * SSD_HEADS

LANE = 128
TOK_TILE = 256
FF_TILE = 512
MOD_ROWS = 8
VMEM_LIMIT = 56 * 1024 * 1024

C_PA = 0
C_PB = C_PA + NA_IN
PB_W = SWA_HEADS * LANE + 2 * SWA_KV_HEADS * HEAD_DIM
C_CQ = C_PB + PB_W
C_CKV = C_CQ + MLA_Q_LORA
C_KR = C_CKV + MLA_KV_LORA
C_Z = C_KR + MLA_HEADS * LANE
C_XBC = C_Z + SSD_INNER
C_DT = C_XBC + SSD_CONV_CH
N_EXT = C_DT + LANE


def _dot(a, b):
    return jnp.dot(a, b, preferred_element_type=F32)


def _dot_nt(a, b):
    return lax.dot_general(a, b, (((1,), (1,)), ((), ())), preferred_element_type=F32)


def _dot_exact(a, b):
    return jnp.dot(a, b, preferred_element_type=F32, precision=lax.Precision.HIGHEST)


def _rms(x, g):
    return x * lax.rsqrt(jnp.mean(x * x, axis=-1, keepdims=True) + NORM_EPS) * g


def _silu(x):
    return x * jax.nn.sigmoid(x)


def _rope(x, cos, sin, half):
    n = x.shape[0]
    lane = lax.broadcasted_iota(jnp.int32, (n, LANE), 1)
    first = (lane % (2 * half)) < half
    outs = []
    for c in range(x.shape[1] // LANE):
        xc = x[:, c * LANE:(c + 1) * LANE]
        rot = jnp.where(first, pltpu.roll(xc, LANE - half, 1), pltpu.roll(xc, half, 1))
        outs.append(xc * cos + rot * sin)
    return outs[0] if len(outs) == 1 else jnp.concatenate(outs, axis=1)


def _params(n_axes):
    return pltpu.CompilerParams(dimension_semantics=("arbitrary",) * n_axes, vmem_limit_bytes=VMEM_LIMIT)


def _mod_kernel(c_ref, w_ref, b_ref, o_ref):
    o_ref[...] = _dot_exact(_silu(c_ref[...]), w_ref[...]) + b_ref[...]


def _modulation(cvec, w_mod, b_mod):
    depth, d, n6 = w_mod.shape
    tn = 1536
    return pl.pallas_call(
        _mod_kernel,
        grid=(depth, n6 // tn),
        in_specs=[
            pl.BlockSpec((MOD_ROWS, d), lambda l, j: (0, 0)),
            pl.BlockSpec((None, d, tn), lambda l, j: (l, 0, j)),
            pl.BlockSpec((None, 1, tn), lambda l, j: (l, 0, j)),
        ],
        out_specs=pl.BlockSpec((None, MOD_ROWS, tn), lambda l, j: (l, 0, j)),
        out_shape=jax.ShapeDtypeStruct((depth, MOD_ROWS, n6), F32),
        compiler_params=_params(2),
        name="modulation",
    )(cvec, w_mod, b_mod.reshape(depth, 1, n6))


def _mod_row(t, tiles_per_batch):
    return jnp.where(t % tiles_per_batch == 0, 0, 1 + t // tiles_per_batch)


def _inproj_kernel(h_ref, mod_ref, g1_ref, w_ref, scos_ref, ssin_ref, mcos_ref, msin_ref,
                   gq_ref, gkv_ref, wuq_ref, wk_ref, wv_ref,
                   pa_ref, pb_ref, qm_ref, km_ref, vm_ref, pz_ref, pxbc_ref, dt_ref):
    xn = _rms(h_ref[...], g1_ref[...])
    xm = (xn * (1.0 + mod_ref[1:2, :]) + mod_ref[0:1, :]).astype(BF16)

    def proj(lo, hi):
        return _dot(xm, w_ref[:, lo:hi])

    pa_ref[...] = proj(C_PA, C_PB).astype(BF16)

    n_qk = (SWA_HEADS + 1) * LANE
    qk = _rope(proj(C_PB, C_PB + n_qk), scos_ref[...], ssin_ref[...], HEAD_DIM // 4)
    pb_ref[:, 0:n_qk] = qk.astype(BF16)
    pb_ref[:, n_qk:PB_W] = proj(C_PB + n_qk, C_CQ).astype(BF16)

    mcos = mcos_ref[...]
    msin = msin_ref[...]
    cq = _rms(proj(C_CQ, C_CKV), gq_ref[...]).astype(BF16)
    qm_ref[...] = (_rope(_dot(cq, wuq_ref[...]), mcos, msin, MLA_ROPE // 4) * MLA_Q_SCALE).astype(BF16)
    ckv = _rms(proj(C_CKV, C_KR), gkv_ref[...]).astype(BF16)
    kr = _rope(proj(C_KR, C_Z), mcos, msin, MLA_ROPE // 4)
    km_ref[...] = (_dot(ckv, wk_ref[...]) + kr).astype(BF16)
    vm_ref[...] = _dot(ckv, wv_ref[...]).T.astype(BF16)

    pz_ref[...] = proj(C_Z, C_XBC)
    pxbc_ref[...] = proj(C_XBC, C_DT)
    dt_ref[...] = proj(C_DT, N_EXT)


def _layer_spec(a, l):
    return pl.BlockSpec((None,) + a.shape[1:], lambda t: (l,) + (0,) * (a.ndim - 1))


def _inproj(h, mods, g1, w_ext, l, tabs, gq, gkv, wuq, wk, wv, tiles_per_batch):
    n_tok = h.shape[0]
    tm = TOK_TILE
    scos, ssin, mcos, msin = tabs
    tok = lambda n: pl.BlockSpec((tm, n), lambda t: (t, 0))
    full = lambda a: pl.BlockSpec(a.shape, lambda t: (0,) * a.ndim)
    tab = pl.BlockSpec((tm, LANE), lambda t: (t % tiles_per_batch, 0))
    widths = (NA_IN, PB_W, MLA_HEADS * LANE, MLA_HEADS * LANE, None, SSD_INNER, SSD_CONV_CH, LANE)
    dtypes = (BF16, BF16, BF16, BF16, BF16, F32, F32, F32)
    tk = _mla_chunk(tiles_per_batch * tm)
    sub = tk // tm
    wv_out = MLA_HEADS * MLA_V
    vt_spec = pl.BlockSpec((None, wv_out, tm), lambda t: (t // sub, 0, t % sub))
    vt_shape = jax.ShapeDtypeStruct((n_tok // tk, wv_out, tk), BF16)
    return pl.pallas_call(
        _inproj_kernel,
        grid=(n_tok // tm,),
        in_specs=[
            tok(D_MODEL),
            pl.BlockSpec((None, 6, D_MODEL), lambda t: (_mod_row(t, tiles_per_batch), 0, 0)),
            full(g1), _layer_spec(w_ext, l), tab, tab, tab, tab, full(gq), full(gkv), full(wuq), full(wk), full(wv),
        ],
        out_specs=[vt_spec if n is None else tok(n) for n in widths],
        out_shape=[vt_shape if n is None else jax.ShapeDtypeStruct((n_tok, n), dt) for n, dt in zip(widths, dtypes)],
        compiler_params=_params(1),
        name="inproj",
    )(h, mods, g1, w_ext, scos, ssin, mcos, msin, gq, gkv, wuq, wk, wv)


def _na_kernel(q_ref, k_ref, v_ref, tab_ref, o_ref, *, rows, lc):
    i = pl.program_id(1)
    scale = HEAD_DIM ** -0.5
    nq = q_ref.shape[0]
    q = q_ref[...]
    lane = lax.broadcasted_iota(jnp.int32, (nq, NA_HEADS * HEAD_DIM), 1)
    k_ctx = k_ref[0:lc, :]
    v_ctx = v_ref[0:lc, :]

    def attend(k_win, v_win):
        out = jnp.zeros((nq, NA_HEADS * HEAD_DIM), F32)
        scores = []
        for h in range(NA_HEADS):
            qh = jnp.where((lane // HEAD_DIM) == h, q, jnp.zeros_like(q))
            scores.append((_dot_nt(qh, k_ctx), None if k_win is None else _dot_nt(qh, k_win)))
        for h, (s_c, s_w) in enumerate(scores):
            in_head = (lane // HEAD_DIM) == h
            s_c = s_c * scale
            m = jnp.max(s_c, axis=-1, keepdims=True)
            if k_win is not None:
                s_w = s_w * scale + tab_ref[h]
                m = jnp.maximum(m, jnp.max(s_w, axis=-1, keepdims=True))
                p_w = jnp.exp(s_w - m)
            p_c = jnp.exp(s_c - m)
            den = jnp.sum(p_c, axis=-1, keepdims=True)
            o = _dot(p_c.astype(BF16), v_ctx)
            if k_win is not None:
                den = den + jnp.sum(p_w, axis=-1, keepdims=True)
                o = o + _dot(p_w.astype(BF16), v_win)
            out = jnp.where(in_head, o / den, out)
        o_ref[...] = out.astype(o_ref.dtype)

    @pl.when(i == 0)
    def _():
        attend(None, None)

    @pl.when(i > 0)
    def _():
        r0 = (i - 1) * NA_QROWS
        ws = jnp.clip(r0 - WIN_H // 2, 0, rows - NA_KROWS)
        start = pl.multiple_of(lc + ws * GRID_W, GRID_W)
        nk = NA_KROWS * GRID_W
        attend(k_ref[pl.ds(start, nk), :], v_ref[pl.ds(start, nk), :])


def _na_attention(pa, table, rows, lc):
    bsz, n_tok, _ = pa.shape
    nq = NA_QROWS * GRID_W
    groups = rows // NA_QROWS
    w = NA_HEADS * HEAD_DIM

    def cfg(b, i):
        g = i - 1
        return (jnp.where(g <= 0, 0, jnp.where(g == groups - 1, 2, 1)), 0, 0, 0)

    return pl.pallas_call(
        functools.partial(_na_kernel, rows=rows, lc=lc),
        grid=(bsz, 1 + groups),
        in_specs=[
            pl.BlockSpec((None, nq, w), lambda b, i: (b, i, 0)),
            pl.BlockSpec((None, n_tok, w), lambda b, i: (b, 0, 1)),
            pl.BlockSpec((None, n_tok, w), lambda b, i: (b, 0, 2)),
            pl.BlockSpec((None, NA_HEADS, nq, NA_KROWS * GRID_W), cfg),
        ],
        out_specs=pl.BlockSpec((None, nq, w), lambda b, i: (b, i, 0)),
        out_shape=jax.ShapeDtypeStruct((bsz, n_tok, w), BF16),
        compiler_params=_params(2),
        name="na_attention",
    )(pa, pa, pa, table)


def _na_bias_tables(rpb, rows):
    groups = rows // NA_QROWS
    i = np.arange(NA_QROWS)
    j = np.arange(NA_KROWS)
    col = np.arange(GRID_W)
    cstart = np.clip(col - WIN_W // 2, 0, GRID_W - WIN_W)
    ok_c = (col[None, :] >= cstart[:, None]) & (col[None, :] < cstart[:, None] + WIN_W)
    dc = np.clip(col[None, :] - col[:, None] + (WIN_W - 1), 0, 2 * WIN_W - 2)
    sel_c = (dc[:, None, :] == np.arange(2 * WIN_W - 1)[None, :, None]).astype(np.float32)
    sel_r, ok = [], []
    for g in (0, 1, groups - 1):
        r = g * NA_QROWS + i
        kr = np.clip(g * NA_QROWS - WIN_H // 2, 0, rows - NA_KROWS) + j
        rstart = np.clip(r - WIN_H // 2, 0, rows - WIN_H)
        ok_r = (kr[None, :] >= rstart[:, None]) & (kr[None, :] < rstart[:, None] + WIN_H)
        dr = np.clip(kr[None, :] - r[:, None] + (WIN_H - 1), 0, 2 * WIN_H - 2)
        sel_r.append((dr[:, None, :] == np.arange(2 * WIN_H - 1)[None, :, None]).astype(np.float32))
        ok.append(ok_r[:, None, :, None] & ok_c[None, :, None, :])
    hi = lax.Precision.HIGHEST
    t = jnp.einsum("ciaj,lhab->lchijb", np.stack(sel_r), rpb, precision=hi)
    t = jnp.einsum("lchijb,qbk->lchiqjk", t, sel_c, precision=hi)
    t = jnp.where(np.stack(ok)[None, :, None], t, NEG_INF)
    return t.reshape(rpb.shape[0], 3, NA_HEADS, NA_QROWS * GRID_W, NA_KROWS * GRID_W)


def _swa_kernel(sink_ref, q_ref, k_ref, v_ref, o_ref, *, nblk, lc):
    i = pl.program_id(1)
    scale = HEAD_DIM ** -0.5
    blk = SWA_BLOCK
    rep = SWA_HEADS // SWA_KV_HEADS
    k_ctx = k_ref[0:lc, :]
    v_ctx = v_ref[0:lc, :]
    row = lax.broadcasted_iota(jnp.int32, (rep * blk, 1), 0)

    def attend(k_win, v_win, valid):
        outs = []
        scores = []
        for g in range(SWA_KV_HEADS):
            qq = jnp.concatenate([q_ref[:, (rep * g + r) * LANE:(rep * g + r + 1) * LANE] for r in range(rep)], axis=0)
            scores.append((_dot_nt(qq, k_ctx), None if k_win is None else _dot_nt(qq, k_win)))
        for g, (s_c, s_w) in enumerate(scores):
            sink = jnp.where(row < blk, sink_ref[rep * g], sink_ref[rep * g + 1])
            s_c = s_c * scale
            m = jnp.maximum(jnp.max(s_c, axis=-1, keepdims=True), sink)
            if k_win is not None:
                s_w = jnp.where(valid, s_w * scale, NEG_INF)
                m = jnp.maximum(m, jnp.max(s_w, axis=-1, keepdims=True))
                p_w = jnp.exp(s_w - m)
            p_c = jnp.exp(s_c - m)
            den = jnp.sum(p_c, axis=-1, keepdims=True) + jnp.exp(sink - m)
            o = _dot(p_c.astype(BF16), v_ctx)
            if k_win is not None:
                den = den + jnp.sum(p_w, axis=-1, keepdims=True)
                o = o + _dot(p_w.astype(BF16), v_win)
            outs.append(o / den)
        lane = lax.broadcasted_iota(jnp.int32, (blk, LANE), 1)
        lo = lane < HEAD_DIM
        o_ref[:, 0:LANE] = jnp.where(lo, outs[0][0:blk], outs[1][0:blk]).astype(o_ref.dtype)
        o_ref[:, LANE:2 * LANE] = jnp.where(lo, outs[0][blk:2 * blk], outs[1][blk:2 * blk]).astype(o_ref.dtype)

    n_ctx_blk = lc // blk

    @pl.when(i < n_ctx_blk)
    def _():
        attend(None, None, None)

    @pl.when(i >= n_ctx_blk)
    def _():
        n = i - n_ctx_blk
        wb = jnp.clip(n - 1, 0, nblk - 3)
        start = pl.multiple_of(lc + wb * blk, blk)
        iq = lax.broadcasted_iota(jnp.int32, (rep * blk, 3 * blk), 0) % blk
        ik = lax.broadcasted_iota(jnp.int32, (rep * blk, 3 * blk), 1)
        dist = (n - wb) * blk + iq - ik
        valid = jnp.abs(dist) <= SWA_WINDOW
        attend(k_ref[pl.ds(start, 3 * blk), :], v_ref[pl.ds(start, 3 * blk), :], valid)


def _swa_attention(pb, sink, lc):
    bsz, n_tok, _ = pb.shape
    blk = SWA_BLOCK
    nblk = (n_tok - lc) // blk
    qw = SWA_HEADS * LANE
    kw = SWA_KV_HEADS * HEAD_DIM
    return pl.pallas_call(
        functools.partial(_swa_kernel, nblk=nblk, lc=lc),
        grid=(bsz, n_tok // blk),
        in_specs=[
            pl.BlockSpec(memory_space=pltpu.SMEM),
            pl.BlockSpec((None, blk, qw), lambda b, i: (b, i, 0)),
            pl.BlockSpec((None, n_tok, kw), lambda b, i: (b, 0, qw // kw)),
            pl.BlockSpec((None, n_tok, kw), lambda b, i: (b, 0, qw // kw + 1)),
        ],
        out_specs=pl.BlockSpec((None, blk, SWA_HEADS * HEAD_DIM), lambda b, i: (b, i, 0)),
        out_shape=jax.ShapeDtypeStruct((bsz, n_tok, SWA_HEADS * HEAD_DIM), BF16),
        compiler_params=_params(2),
        name="swa_attention",
    )(sink, pb, pb, pb)


def _mla_chunk(n_tok):
    return next(c for c in (3 * TOK_TILE, 2 * TOK_TILE, TOK_TILE) if n_tok % c == 0)


def _reduce_rows(x, fn):
    n, w = x.shape
    for k in (8, 4, 2):
        while n > 8 and n % (8 * k) == 0:
            x = fn(x.reshape(k, n // k, w), axis=0)
            n //= k
    return fn(x, axis=0, keepdims=True)


def _mla_kernel(q_ref, k_ref, vt_ref, o_ref, qbd_ref, sa_ref, sb_ref, *, lc):
    i = pl.program_id(1)
    n_chunks, _, tk = vt_ref.shape
    tq = q_ref.shape[0]
    pairs = MLA_HEADS // 2

    lane = lax.broadcasted_iota(jnp.int32, (tq, 2 * LANE), 1)
    for j in range(pairs):
        qp = q_ref[:, 2 * j * LANE:2 * (j + 1) * LANE]
        qbd_ref[j, 0:tq, :] = jnp.where(lane < LANE, qp, jnp.zeros_like(qp))
        qbd_ref[j, tq:2 * tq, :] = jnp.where(lane >= LANE, qp, jnp.zeros_like(qp))

    def attend(n_trips, width):
        assert n_trips % 2 == 1

        def scores(c, buf):
            st = pl.multiple_of(c * tk, tk)
            maxima = []
            for j in range(pairs):
                s = _dot_nt(k_ref[pl.ds(st, width), 2 * j * LANE:2 * (j + 1) * LANE], qbd_ref[j])
                buf[j, 0:width, :] = s
                maxima.append(_reduce_rows(s, jnp.max))
            return tuple(maxima)

        def update(c, buf, maxima, state):
            new = []
            for j, (m_old, den, acc_a, acc_b) in enumerate(state):
                m_new = jnp.maximum(m_old, maxima[j])
                alpha = jnp.exp2(m_old - m_new)
                p = jnp.exp2(buf[j, 0:width, :] - m_new)
                den = alpha * den + _reduce_rows(p, jnp.sum)
                pb = p.astype(BF16)
                pv_a = _dot(vt_ref[c, 2 * j * MLA_V:(2 * j + 1) * MLA_V, 0:width], pb[:, 0:tq])
                pv_b = _dot(vt_ref[c, (2 * j + 1) * MLA_V:(2 * j + 2) * MLA_V, 0:width], pb[:, tq:2 * tq])
                acc_a = alpha[:, 0:tq] * acc_a + pv_a
                acc_b = alpha[:, tq:2 * tq] * acc_b + pv_b
                new.append((m_new, den, acc_a, acc_b))
            return tuple(new)

        def body(t, carry):
            maxima, state = carry
            nxt = scores(2 * t + 1, sb_ref)
            state = update(2 * t, sa_ref, maxima, state)
            maxima = scores(2 * t + 2, sa_ref)
            state = update(2 * t + 1, sb_ref, nxt, state)
            return maxima, state

        init = (jnp.full((1, 2 * tq), NEG_INF, F32), jnp.zeros((1, 2 * tq), F32),
                jnp.zeros((MLA_V, tq), F32), jnp.zeros((MLA_V, tq), F32))
        maxima, state = lax.fori_loop(0, n_trips // 2, body, (scores(0, sa_ref), (init,) * pairs))
        state = update(n_trips - 1, sa_ref, maxima, state)
        out = []
        for _, den, acc_a, acc_b in state:
            out += [acc_a / den[:, 0:tq], acc_b / den[:, tq:2 * tq]]
        o_ref[...] = jnp.concatenate(out, axis=0).T.astype(o_ref.dtype)

    @pl.when(i == 0)
    def _():
        attend(1, lc)

    @pl.when(i > 0)
    def _():
        attend(n_chunks, tk)


def _mla_attention(qm, km, vt, lc):
    bsz, n_tok, qw = qm.shape
    tq = TOK_TILE
    assert lc == tq
    wv = MLA_HEADS * MLA_V
    tk = vt.shape[-1]
    return pl.pallas_call(
        functools.partial(_mla_kernel, lc=lc),
        grid=(bsz, n_tok // tq),
        in_specs=[
            pl.BlockSpec((None, tq, qw), lambda b, i: (b, i, 0)),
            pl.BlockSpec((None, n_tok, qw), lambda b, i: (b, 0, 0)),
            pl.BlockSpec((None, n_tok // tk, wv, tk), lambda b, i: (b, 0, 0, 0)),
        ],
        out_specs=pl.BlockSpec((None, tq, wv), lambda b, i: (b, i, 0)),
        out_shape=jax.ShapeDtypeStruct((bsz, n_tok, wv), BF16),
        scratch_shapes=[pltpu.VMEM((MLA_HEADS // 2, 2 * tq, 2 * LANE), BF16),
                        pltpu.VMEM((MLA_HEADS // 2, tk, 2 * tq), F32), pltpu.VMEM((MLA_HEADS // 2, tk, 2 * tq), F32)],
        compiler_params=_params(2),
        name="mla_attention",
    )(qm, km, vt.reshape(bsz, n_tok // tk, wv, tk))


def _ssd_chunk(step, reverse, n_ctx, n_chunks):
    if not reverse:
        return step
    return jnp.where(step < n_ctx, n_ctx - 1 - step, n_chunks + n_ctx - 1 - step)


def _expand_heads(v, base):
    q = v.shape[0]
    lane = lax.broadcasted_iota(jnp.int32, (q, SSD_INNER), 1)
    out = jnp.broadcast_to(v[:, base:base + 1], (q, SSD_INNER))
    for h in range(1, SSD_HEADS):
        out = jnp.where(lane >= h * HEAD_DIM, jnp.broadcast_to(v[:, base + h:base + h + 1], (q, SSD_INNER)), out)
    return out


def _ssd_direction(ext_ref, dt_ref, cw_ref, cb_ref, dtb_ref, alog_ref, hs, reverse):
    q = SSD_CHUNK
    pad = SSD_CONV // 2
    base = SSD_HEADS if reverse else 0
    last = 0 if reverse else q - 1

    acc = ext_ref[pl.ds(SSD_HALO - pad, q), :] * cw_ref[0:1, :]
    for t in range(1, SSD_CONV):
        acc = acc + ext_ref[pl.ds(SSD_HALO - pad + t, q), :] * cw_ref[t:t + 1, :]
    xbc = _silu(acc + cb_ref[...])
    x = xbc[:, 0:SSD_INNER]

    dt_in = dt_ref[...] + dtb_ref[...]
    dt = jnp.maximum(dt_in, 0.0) + jnp.log1p(jnp.exp(-jnp.abs(dt_in)))
    da = dt * (-jnp.exp(alog_ref[...]))
    ii = lax.broadcasted_iota(jnp.int32, (q, q), 0)
    jj = lax.broadcasted_iota(jnp.int32, (q, q), 1)
    lower = (jj <= ii).astype(F32)
    upper = (jj >= ii).astype(F32)
    tri, tri_t = (upper, lower) if reverse else (lower, upper)
    cum = _dot_exact(tri, da)
    cum_t = _dot_exact(da.T, tri_t)
    causal = (jj >= ii) if reverse else (jj <= ii)

    dtx = _expand_heads(dt, base)
    cum_x = _expand_heads(cum, base)
    xdt = x * dtx
    xdt_b = xdt.astype(BF16)
    cum_last = cum_x[last:last + 1, :]
    xdtd_b = (xdt * jnp.exp(cum_last - cum_x)).astype(BF16)
    lane = lax.broadcasted_iota(jnp.int32, (q, SSD_INNER), 1)
    hs_b = hs.astype(BF16)

    y = jnp.zeros((q, SSD_INNER), F32)
    y_off = []
    new_state = []
    for g in range(SSD_GROUPS):
        bm = xbc[:, SSD_INNER + g * SSD_STATE:SSD_INNER + (g + 1) * SSD_STATE]
        cm = xbc[:, SSD_INNER + (SSD_GROUPS + g) * SSD_STATE:SSD_INNER + (SSD_GROUPS + g + 1) * SSD_STATE]
        cm_b = cm.astype(BF16)
        cb = _dot_nt(cm_b, bm.astype(BF16))
        for h in range(g * SSD_HEADS // SSD_GROUPS, (g + 1) * SSD_HEADS // SSD_GROUPS):
            col = cum[:, base + h:base + h + 1]
            rowv = cum_t[base + h:base + h + 1, :]
            seg = jnp.exp(jnp.where(causal, col - rowv, -jnp.inf))
            yd = _dot((cb * seg).astype(BF16), xdt_b)
            y = jnp.where((lane // HEAD_DIM) == h, yd, y)
        y_off.append(_dot(cm_b, hs_b))
        new_state.append(_dot(bm.T.astype(BF16), xdtd_b))
    half = lane < SSD_INNER // SSD_GROUPS
    y = y + jnp.where(half, y_off[0], y_off[1]) * jnp.exp(cum_x)
    hs_new = hs * jnp.exp(cum_last) + jnp.where(half, new_state[0], new_state[1])
    return y, x, hs_new


def _ssd_kernel(xf_ref, pf_ref, nf_ref, dtf_ref, xb_ref, pb_ref, nb_ref, dtr_ref,
                cw_ref, cb_ref, dtb_ref, alog_ref, skip_ref, yf_ref, yb_ref,
                extf_ref, extb_ref, hsf_ref, hsb_ref, *, n_ctx, n_chunks):
    step = pl.program_id(1)
    q = SSD_CHUNK

    @pl.when(step == 0)
    def _():
        hsf_ref[...] = jnp.zeros_like(hsf_ref)
        hsb_ref[...] = jnp.zeros_like(hsb_ref)

    for reverse, ext_ref, x_ref, p_ref, n_ref in ((False, extf_ref, xf_ref, pf_ref, nf_ref),
                                                  (True, extb_ref, xb_ref, pb_ref, nb_ref)):
        chunk = _ssd_chunk(step, reverse, n_ctx, n_chunks)
        has_prev = jnp.logical_and(chunk != 0, chunk != n_ctx)
        has_next = jnp.logical_and(chunk != n_ctx - 1, chunk != n_chunks - 1)
        ext_ref[0:SSD_HALO, :] = jnp.where(has_prev, p_ref[...], 0.0)
        ext_ref[SSD_HALO:SSD_HALO + q, :] = x_ref[...]
        ext_ref[SSD_HALO + q:, :] = jnp.where(has_next, n_ref[...], 0.0)

    params = (cw_ref, cb_ref, dtb_ref, alog_ref)
    y_f, x_f, hs_f = _ssd_direction(extf_ref, dtf_ref, *params, hsf_ref[...], False)
    y_b, _, hs_b = _ssd_direction(extb_ref, dtr_ref, *params, hsb_ref[...], True)
    yf_ref[...] = x_f * skip_ref[...] + y_f
    yb_ref[...] = y_b
    hsf_ref[...] = hs_f
    hsb_ref[...] = hs_b


def _ssd_scan(pxbc, dtp, conv_w, conv_b, dt_bias, a_log, skip, lc):
    bsz, n_tok, _ = pxbc.shape
    q = SSD_CHUNK
    n_chunks = n_tok // q
    n_ctx = lc // q
    per = q // SSD_HALO
    full = lambda a: pl.BlockSpec(a.shape, lambda b, s: (0,) * a.ndim)
    in_specs, args, out_specs = [], [], []
    for reverse in (False, True):
        chunk = functools.partial(_ssd_chunk, reverse=reverse, n_ctx=n_ctx, n_chunks=n_chunks)
        tok = lambda n, chunk=chunk: pl.BlockSpec((None, q, n), lambda b, s: (b, chunk(s), 0))
        prev = pl.BlockSpec((None, SSD_HALO, SSD_CONV_CH),
                            lambda b, s, chunk=chunk: (b, jnp.maximum(chunk(s) * per - 1, 0), 0))
        nxt = pl.BlockSpec((None, SSD_HALO, SSD_CONV_CH),
                           lambda b, s, chunk=chunk: (b, jnp.minimum((chunk(s) + 1) * per, n_tok // SSD_HALO - 1), 0))
        in_specs += [tok(SSD_CONV_CH), prev, nxt, tok(LANE)]
        args += [pxbc, pxbc, pxbc, dtp]
        out_specs.append(tok(SSD_INNER))
    in_specs += [full(conv_w), full(conv_b), full(dt_bias), full(a_log), full(skip)]
    args += [conv_w, conv_b, dt_bias, a_log, skip]
    ext = pltpu.VMEM((q + 2 * SSD_HALO, SSD_CONV_CH), F32)
    state = pltpu.VMEM((SSD_STATE, SSD_INNER), F32)
    return pl.pallas_call(
        functools.partial(_ssd_kernel, n_ctx=n_ctx, n_chunks=n_chunks),
        grid=(bsz, n_chunks),
        in_specs=in_specs,
        out_specs=out_specs,
        out_shape=[jax.ShapeDtypeStruct((bsz, n_tok, SSD_INNER), F32)] * 2,
        scratch_shapes=[ext, ext, state, state],
        compiler_params=_params(2),
        name="ssd_scan",
    )(*args)


def _outmlp_kernel(h_ref, oa_ref, ob_ref, om_ref, yf_ref, yb_ref, z_ref, gn_ref, mod_ref, wo_ref, g2_ref, w1_ref,
                   w2_ref, o_ref):
    od = _rms((yf_ref[...] + yb_ref[...]) * _silu(z_ref[...]), gn_ref[...]).astype(BF16)
    mix = jnp.concatenate([oa_ref[...], ob_ref[...], om_ref[...], od], axis=1)
    h1 = h_ref[...] + mod_ref[2:3, :] * _dot(mix, wo_ref[...])
    xm = (_rms(h1, g2_ref[...]) * (1.0 + mod_ref[4:5, :]) + mod_ref[3:4, :]).astype(BF16)
    acc = jnp.zeros(h1.shape, F32)
    for c in range(D_FF // FF_TILE):
        a = jnp.maximum(_dot(xm, w1_ref[:, c * FF_TILE:(c + 1) * FF_TILE]), 0.0)
        acc = acc + _dot((a * a).astype(BF16), w2_ref[c * FF_TILE:(c + 1) * FF_TILE, :])
    o_ref[...] = h1 + mod_ref[5:6, :] * acc


def _outmlp(h, oa, ob, om, yf, yb, pz, gnorm, mods, wo, g2, w1, w2, l, tiles_per_batch):
    n_tok = h.shape[0]
    tm = TOK_TILE
    tok = lambda n: pl.BlockSpec((tm, n), lambda t: (t, 0))
    full = lambda a: pl.BlockSpec(a.shape, lambda t: (0,) * a.ndim)
    return pl.pallas_call(
        _outmlp_kernel,
        grid=(n_tok // tm,),
        in_specs=[
            tok(D_MODEL), tok(GROUP_W), tok(GROUP_W), tok(GROUP_W), tok(GROUP_W), tok(GROUP_W), tok(GROUP_W),
            full(gnorm),
            pl.BlockSpec((None, 6, D_MODEL), lambda t: (_mod_row(t, tiles_per_batch), 0, 0)),
            _layer_spec(wo, l), full(g2), _layer_spec(w1, l), _layer_spec(w2, l),
        ],
        out_specs=tok(D_MODEL),
        out_shape=jax.ShapeDtypeStruct((n_tok, D_MODEL), F32),
        compiler_params=_params(1),
        name="outproj_mlp",
    )(h, oa, ob, om, yf, yb, pz, gnorm, mods, wo, g2, w1, w2)


def _final_norm_kernel(h_ref, g_ref, o_ref):
    o_ref[...] = _rms(h_ref[...], g_ref[...])


def _final_norm(h, g, lc):
    bsz, n_tok, d = h.shape
    tm = TOK_TILE
    off = lc // tm
    return pl.pallas_call(
        _final_norm_kernel,
        grid=(bsz, (n_tok - lc) // tm),
        in_specs=[pl.BlockSpec((None, tm, d), lambda b, t: (b, t + off, 0)), pl.BlockSpec((1, d), lambda b, t: (0, 0))],
        out_specs=pl.BlockSpec((None, tm, d), lambda b, t: (b, t, 0)),
        out_shape=jax.ShapeDtypeStruct((bsz, n_tok - lc, d), F32),
        compiler_params=_params(2),
        name="final_norm",
    )(h, g)


def _rope_tables(s, lc):
    t = jnp.arange(s)
    pos = (t // GRID_W).astype(F32), (t % GRID_W).astype(F32)
    lane = jnp.arange(LANE)

    def table(d, width, active):
        nf = width // 4
        inv = 1.0 / (ROPE_BASE ** (jnp.arange(nf, dtype=F32) / nf))
        half = d // (width // 2)
        sub = (d % (width // 2)) // nf
        f = d % nf
        ang = jnp.where(half[None, :] == 0, pos[0][:, None], pos[1][:, None]) * inv[f][None, :]
        cos = jnp.where(active[None, :], jnp.cos(ang), 1.0)
        sin = jnp.where(active[None, :], jnp.where(sub[None, :] == 0, -1.0, 1.0) * jnp.sin(ang), 0.0)
        ident = jnp.ones((lc, LANE), F32), jnp.zeros((lc, LANE), F32)
        return jnp.concatenate([ident[0], cos]), jnp.concatenate([ident[1], sin])

    scos, ssin = table(lane % HEAD_DIM, HEAD_DIM, jnp.ones((LANE,), bool))
    m_act = (lane >= MLA_NOPE) & (lane < MLA_NOPE + MLA_ROPE)
    mcos, msin = table(jnp.clip(lane - MLA_NOPE, 0, MLA_ROPE - 1), MLA_ROPE, m_act)
    return scos, ssin, mcos, msin


def _extend_w_in(w_in):
    depth, d, _ = w_in.shape
    z = lambda n: jnp.zeros((depth, d, n), w_in.dtype)
    o_swa, o_mla, o_ssd = NA_IN, NA_IN + SWA_IN, NA_IN + SWA_IN + MLA_IN
    cols = [w_in[..., 0:NA_IN]]
    rep = SWA_HEADS // SWA_KV_HEADS
    for hq in range(SWA_HEADS):
        qh = w_in[..., o_swa + hq * HEAD_DIM:o_swa + (hq + 1) * HEAD_DIM]
        cols += [qh, z(HEAD_DIM)] if hq // rep == 0 else [z(HEAD_DIM), qh]
    cols.append(w_in[..., o_swa + SWA_HEADS * HEAD_DIM:o_mla])
    cols.append(w_in[..., o_mla:o_mla + MLA_Q_LORA + MLA_KV_LORA])
    kr = w_in[..., o_mla + MLA_Q_LORA + MLA_KV_LORA:o_ssd]
    for _ in range(MLA_HEADS):
        cols += [z(MLA_NOPE), kr, z(LANE - MLA_NOPE - MLA_ROPE)]
    cols.append(w_in[..., o_ssd:o_ssd + SSD_INNER + SSD_CONV_CH])
    cols += [w_in[..., o_ssd + SSD_INNER + SSD_CONV_CH:], z(LANE - 2 * SSD_HEADS)]
    w = jnp.concatenate(cols, axis=-1).astype(BF16)
    assert w.shape[-1] == N_EXT
    return w


def _mla_weights(w_uq, w_ukv):
    depth = w_uq.shape[0]
    dq = MLA_NOPE + MLA_ROPE
    uq, uk, uv = [], [], []
    for h in range(MLA_HEADS):
        uq += [w_uq[..., h * dq:(h + 1) * dq], jnp.zeros((depth, MLA_Q_LORA, LANE - dq), w_uq.dtype)]
        uk += [w_ukv[..., h * LANE:h * LANE + MLA_NOPE], jnp.zeros((depth, MLA_KV_LORA, LANE - MLA_NOPE), w_ukv.dtype)]
        uv.append(w_ukv[..., h * LANE + MLA_NOPE:(h + 1) * LANE])
    cat = lambda xs: jnp.concatenate(xs, axis=-1).astype(BF16)
    return cat(uq), cat(uk), cat(uv)


def _permute_w_out(w_out):
    blocks = [w_out[:, :GROUP_W]]
    for h in (0, 2, 1, 3):
        blocks.append(w_out[:, GROUP_W + h * HEAD_DIM:GROUP_W + (h + 1) * HEAD_DIM])
    blocks.append(w_out[:, 2 * GROUP_W:])
    return jnp.concatenate(blocks, axis=1).astype(BF16)


def kernel(x, c, ctx, c_ctx, w_mod, b_mod, g_norm1, w_in, na_rpb, swa_sink, mla_g_q, mla_g_kv, mla_w_uq, mla_w_ukv,
           ssd_conv_w, ssd_conv_b, ssd_dt_bias, ssd_a_log, ssd_d, ssd_g_norm, w_out, g_norm2, w_mlp1, w_mlp2, g_final):
    bsz, s, d = x.shape
    lc = ctx.shape[1]
    depth = w_in.shape[0]
    n_tok = lc + s
    rows = s // GRID_W
    tiles_per_batch = n_tok // TOK_TILE
    assert d == D_MODEL and lc == TOK_TILE and s % (NA_QROWS * GRID_W) == 0 and rows >= NA_KROWS + NA_QROWS
    assert bsz + 1 <= MOD_ROWS

    cvec = jnp.concatenate([c_ctx[None], c, jnp.zeros((MOD_ROWS - 1 - bsz, d), F32)], axis=0)
    mods = _modulation(cvec, w_mod, b_mod).reshape(depth, MOD_ROWS, 6, d)

    tabs = _rope_tables(s, lc)
    na_tables = _na_bias_tables(na_rpb, rows)
    w_ext = _extend_w_in(w_in)
    wuq, wk, wv = _mla_weights(mla_w_uq, mla_w_ukv)
    wo = _permute_w_out(w_out)
    w1 = w_mlp1.astype(BF16)
    w2 = w_mlp2.astype(BF16)
    pad_lanes = lambda a: jnp.pad(a.reshape(depth, 1, -1), ((0, 0), (0, 0), (0, LANE - a.shape[-1] * a.shape[-2])))
    dt_bias = pad_lanes(ssd_dt_bias)
    a_log = pad_lanes(ssd_a_log)
    conv_w = jnp.pad(ssd_conv_w, ((0, 0), (0, SSD_HALO - SSD_CONV), (0, 0)))
    skip = jnp.repeat(ssd_d, HEAD_DIM, axis=-1)

    h = jnp.concatenate([ctx, x], axis=1).reshape(bsz * n_tok, d)
    for l in range(depth):
        row = lambda a: a[l].reshape(1, -1)
        pa, pb, qm, km, vm, pz, pxbc, dtp = _inproj(
            h, mods[l], row(g_norm1), w_ext, l, tabs, row(mla_g_q), row(mla_g_kv), wuq[l], wk[l], wv[l],
            tiles_per_batch)
        per_batch = lambda a: a.reshape(bsz, n_tok, a.shape[-1])
        oa = _na_attention(per_batch(pa), na_tables[l], rows, lc)
        ob = _swa_attention(per_batch(pb), swa_sink[l], lc)
        om = _mla_attention(per_batch(qm), per_batch(km), vm, lc)
        yf, yb = _ssd_scan(per_batch(pxbc), per_batch(dtp), conv_w[l], row(ssd_conv_b), dt_bias[l], a_log[l],
                           row(skip), lc)
        flat = lambda a: a.reshape(bsz * n_tok, a.shape[-1])
        h = _outmlp(h, flat(oa), flat(ob), flat(om), flat(yf), flat(yb), pz, row(ssd_g_norm), mods[l], wo,
                    row(g_norm2), w1, w2, l, tiles_per_batch)
    return _final_norm(h.reshape(bsz, n_tok, d), g_final.reshape(1, d), lc)
```

```python
import functools
import math

import jax
import numpy as np
import jax.numpy as jnp
from jax import lax
from jax.experimental import pallas as pl
from jax.experimental.pallas import tpu as pltpu

F32 = jnp.float32
BF16 = jnp.bfloat16

D_MODEL = 1024
GRID_W = 64
HEAD_DIM = 64
D_FF = 4 * D_MODEL
NORM_EPS = 1e-6
ROPE_BASE = 10000.0
NEG_INF = -1e30
GROUP_W = D_MODEL // 4

NA_HEADS = 4
WIN_H = 8
WIN_W = 16
NA_QROWS = 4
NA_KROWS = 12

SWA_HEADS = 4
SWA_KV_HEADS = 2
SWA_WINDOW = 128
SWA_BLOCK = 128

MLA_HEADS = 4
MLA_Q_LORA = 256
MLA_KV_LORA = 128
MLA_NOPE = 64
MLA_ROPE = 32
MLA_V = 64
MLA_Q_SCALE = (MLA_NOPE + MLA_ROPE) ** -0.5 * math.log2(math.e)

SSD_INNER = GROUP_W
SSD_HEADS = 4
SSD_GROUPS = 2
SSD_STATE = 128
SSD_CONV = 5
SSD_CHUNK = 128
SSD_CONV_CH = SSD_INNER + 2 * SSD_GROUPS * SSD_STATE
SSD_HALO = 8

NA_IN = 3 * NA_HEADS * HEAD_DIM
SWA_IN = (SWA_HEADS + 2 * SWA_KV_HEADS) * HEAD_DIM
MLA_IN = MLA_Q_LORA + MLA_KV_LORA + MLA_ROPE
SSD_IN = 2 * SSD_INNER + 2 * SSD_GROUPS * SSD_STATE + 2 * SSD_HEADS

LANE = 128
TOK_TILE = 256
FF_TILE = 512
MOD_ROWS = 8
VMEM_LIMIT = 56 * 1024 * 1024

C_PA = 0
C_PB = C_PA + NA_IN
PB_W = SWA_HEADS * LANE + 2 * SWA_KV_HEADS * HEAD_DIM
C_CQ = C_PB + PB_W
C_CKV = C_CQ + MLA_Q_LORA
C_KR = C_CKV + MLA_KV_LORA
C_Z = C_KR + MLA_HEADS * LANE
C_XBC = C_Z + SSD_INNER
C_DT = C_XBC + SSD_CONV_CH
N_EXT = C_DT + LANE


def _dot(a, b):
    return jnp.dot(a, b, preferred_element_type=F32)


def _dot_nt(a, b):
    return lax.dot_general(a, b, (((1,), (1,)), ((), ())), preferred_element_type=F32)


def _dot_exact(a, b):
    return jnp.dot(a, b, preferred_element_type=F32, precision=lax.Precision.HIGHEST)


def _rms(x, g):
    return x * lax.rsqrt(jnp.mean(x * x, axis=-1, keepdims=True) + NORM_EPS) * g


def _silu(x):
    return x * jax.nn.sigmoid(x)


def _rope(x, cos, sin, half):
    n = x.shape[0]
    lane = lax.broadcasted_iota(jnp.int32, (n, LANE), 1)
    first = (lane % (2 * half)) < half
    outs = []
    for c in range(x.shape[1] // LANE):
        xc = x[:, c * LANE:(c + 1) * LANE]
        rot = jnp.where(first, pltpu.roll(xc, LANE - half, 1), pltpu.roll(xc, half, 1))
        outs.append(xc * cos + rot * sin)
    return outs[0] if len(outs) == 1 else jnp.concatenate(outs, axis=1)


def _params(n_axes):
    return pltpu.CompilerParams(dimension_semantics=("arbitrary",) * n_axes, vmem_limit_bytes=VMEM_LIMIT)


def _mod_kernel(c_ref, w_ref, b_ref, o_ref):
    o_ref[...] = _dot_exact(_silu(c_ref[...]), w_ref[...]) + b_ref[...]


def _modulation(cvec, w_mod, b_mod):
    depth, d, n6 = w_mod.shape
    tn = 1536
    return pl.pallas_call(
        _mod_kernel,
        grid=(depth, n6 // tn),
        in_specs=[
            pl.BlockSpec((MOD_ROWS, d), lambda l, j: (0, 0)),
            pl.BlockSpec((None, d, tn), lambda l, j: (l, 0, j)),
            pl.BlockSpec((None, 1, tn), lambda l, j: (l, 0, j)),
        ],
        out_specs=pl.BlockSpec((None, MOD_ROWS, tn), lambda l, j: (l, 0, j)),
        out_shape=jax.ShapeDtypeStruct((depth, MOD_ROWS, n6), F32),
        compiler_params=_params(2),
        name="modulation",
    )(cvec, w_mod, b_mod.reshape(depth, 1, n6))


def _mod_row(t, tiles_per_batch):
    return jnp.where(t % tiles_per_batch == 0, 0, 1 + t // tiles_per_batch)


def _inproj_kernel(h_ref, mod_ref, g1_ref, w_ref, scos_ref, ssin_ref, mcos_ref, msin_ref,
                   gq_ref, gkv_ref, wuq_ref, wk_ref, wv_ref,
                   pa_ref, pb_ref, qm_ref, km_ref, vm_ref, pz_ref, pxbc_ref, dt_ref):
    xn = _rms(h_ref[...], g1_ref[...])
    xm = (xn * (1.0 + mod_ref[1:2, :]) + mod_ref[0:1, :]).astype(BF16)

    def proj(lo, hi):
        return _dot(xm, w_ref[:, lo:hi])

    pa_ref[...] = proj(C_PA, C_PB).astype(BF16)

    n_qk = (SWA_HEADS + 1) * LANE
    qk = _rope(proj(C_PB, C_PB + n_qk), scos_ref[...], ssin_ref[...], HEAD_DIM // 4)
    pb_ref[:, 0:n_qk] = qk.astype(BF16)
    pb_ref[:, n_qk:PB_W] = proj(C_PB + n_qk, C_CQ).astype(BF16)

    mcos = mcos_ref[...]
    msin = msin_ref[...]
    cq = _rms(proj(C_CQ, C_CKV), gq_ref[...]).astype(BF16)
    qm_ref[...] = (_rope(_dot(cq, wuq_ref[...]), mcos, msin, MLA_ROPE // 4) * MLA_Q_SCALE).astype(BF16)
    ckv = _rms(proj(C_CKV, C_KR), gkv_ref[...]).astype(BF16)
    kr = _rope(proj(C_KR, C_Z), mcos, msin, MLA_ROPE // 4)
    km_ref[...] = (_dot(ckv, wk_ref[...]) + kr).astype(BF16)
    vm_ref[...] = _dot(ckv, wv_ref[...]).T.astype(BF16)

    pz_ref[...] = proj(C_Z, C_XBC)
    pxbc_ref[...] = proj(C_XBC, C_DT)
    dt_ref[...] = proj(C_DT, N_EXT)


def _layer_spec(a, l, mode=None):
    return pl.BlockSpec((None,) + a.shape[1:], lambda t: (l,) + (0,) * (a.ndim - 1), pipeline_mode=mode)


def _inproj(h, mods, g1, w_ext, l, tabs, gq, gkv, wuq, wk, wv, tiles_per_batch):
    n_tok = h.shape[0]
    tm = TOK_TILE
    scos, ssin, mcos, msin = tabs
    tok = lambda n: pl.BlockSpec((tm, n), lambda t: (t, 0))
    full = lambda a: pl.BlockSpec(a.shape, lambda t: (0,) * a.ndim)
    tab = pl.BlockSpec((tm, LANE), lambda t: (t % tiles_per_batch, 0))
    widths = (NA_IN, PB_W, MLA_HEADS * LANE, MLA_HEADS * LANE, None, SSD_INNER, SSD_CONV_CH, LANE)
    dtypes = (BF16, BF16, BF16, BF16, BF16, F32, F32, F32)
    tk = _mla_chunk(tiles_per_batch * tm)
    sub = tk // tm
    wv_out = MLA_HEADS * MLA_V
    vt_spec = pl.BlockSpec((None, wv_out, tm), lambda t: (t // sub, 0, t % sub))
    vt_shape = jax.ShapeDtypeStruct((n_tok // tk, wv_out, tk), BF16)
    return pl.pallas_call(
        _inproj_kernel,
        grid=(n_tok // tm,),
        in_specs=[
            tok(D_MODEL),
            pl.BlockSpec((None, 6, D_MODEL), lambda t: (_mod_row(t, tiles_per_batch), 0, 0)),
            full(g1), _layer_spec(w_ext, l), tab, tab, tab, tab, full(gq), full(gkv), full(wuq), full(wk), full(wv),
        ],
        out_specs=[vt_spec if n is None else tok(n) for n in widths],
        out_shape=[vt_shape if n is None else jax.ShapeDtypeStruct((n_tok, n), dt) for n, dt in zip(widths, dtypes)],
        compiler_params=_params(1),
        name="inproj",
    )(h, mods, g1, w_ext, scos, ssin, mcos, msin, gq, gkv, wuq, wk, wv)


def _na_kernel(q_ref, k_ref, v_ref, tab_ref, o_ref, *, rows, lc):
    i = pl.program_id(1)
    scale = HEAD_DIM ** -0.5
    nq = q_ref.shape[0]
    q = q_ref[...]
    lane = lax.broadcasted_iota(jnp.int32, (nq, NA_HEADS * HEAD_DIM), 1)
    k_ctx = k_ref[0:lc, :]
    v_ctx = v_ref[0:lc, :]

    def attend(k_win, v_win):
        out = jnp.zeros((nq, NA_HEADS * HEAD_DIM), F32)
        scores = []
        for h in range(NA_HEADS):
            qh = jnp.where((lane // HEAD_DIM) == h, q, jnp.zeros_like(q))
            scores.append((_dot_nt(qh, k_ctx), None if k_win is None else _dot_nt(qh, k_win)))
        for h, (s_c, s_w) in enumerate(scores):
            in_head = (lane // HEAD_DIM) == h
            s_c = s_c * scale
            m = jnp.max(s_c, axis=-1, keepdims=True)
            if k_win is not None:
                s_w = s_w * scale + tab_ref[h]
                m = jnp.maximum(m, jnp.max(s_w, axis=-1, keepdims=True))
                p_w = jnp.exp(s_w - m)
            p_c = jnp.exp(s_c - m)
            den = jnp.sum(p_c, axis=-1, keepdims=True)
            o = _dot(p_c.astype(BF16), v_ctx)
            if k_win is not None:
                den = den + jnp.sum(p_w, axis=-1, keepdims=True)
                o = o + _dot(p_w.astype(BF16), v_win)
            out = jnp.where(in_head, o / den, out)
        o_ref[...] = out.astype(o_ref.dtype)

    @pl.when(i == 0)
    def _():
        attend(None, None)

    @pl.when(i > 0)
    def _():
        r0 = (i - 1) * NA_QROWS
        ws = jnp.clip(r0 - WIN_H // 2, 0, rows - NA_KROWS)
        start = pl.multiple_of(lc + ws * GRID_W, GRID_W)
        nk = NA_KROWS * GRID_W
        attend(k_ref[pl.ds(start, nk), :], v_ref[pl.ds(start, nk), :])


def _na_attention(pa, tables, l, rows, lc):
    bsz, n_tok, _ = pa.shape
    nq = NA_QROWS * GRID_W
    groups = rows // NA_QROWS
    w = NA_HEADS * HEAD_DIM

    def cfg(b, i):
        g = i - 1
        return (l, jnp.where(g <= 0, 0, jnp.where(g == groups - 1, 2, 1)), 0, 0, 0)

    return pl.pallas_call(
        functools.partial(_na_kernel, rows=rows, lc=lc),
        grid=(bsz, 1 + groups),
        in_specs=[
            pl.BlockSpec((None, nq, w), lambda b, i: (b, i, 0)),
            pl.BlockSpec((None, n_tok, w), lambda b, i: (b, 0, 1)),
            pl.BlockSpec((None, n_tok, w), lambda b, i: (b, 0, 2)),
            pl.BlockSpec((None, None, NA_HEADS, nq, NA_KROWS * GRID_W), cfg),
        ],
        out_specs=pl.BlockSpec((None, nq, w), lambda b, i: (b, i, 0)),
        out_shape=jax.ShapeDtypeStruct((bsz, n_tok, w), BF16),
        compiler_params=_params(2),
        name="na_attention",
    )(pa, pa, pa, tables)


def _na_bias_tables(rpb, rows):
    groups = rows // NA_QROWS
    i = np.arange(NA_QROWS)
    j = np.arange(NA_KROWS)
    col = np.arange(GRID_W)
    cstart = np.clip(col - WIN_W // 2, 0, GRID_W - WIN_W)
    ok_c = (col[None, :] >= cstart[:, None]) & (col[None, :] < cstart[:, None] + WIN_W)
    dc = np.clip(col[None, :] - col[:, None] + (WIN_W - 1), 0, 2 * WIN_W - 2)
    sel_c = (dc[:, None, :] == np.arange(2 * WIN_W - 1)[None, :, None]).astype(np.float32)
    sel_r, ok = [], []
    for g in (0, 1, groups - 1):
        r = g * NA_QROWS + i
        kr = np.clip(g * NA_QROWS - WIN_H // 2, 0, rows - NA_KROWS) + j
        rstart = np.clip(r - WIN_H // 2, 0, rows - WIN_H)
        ok_r = (kr[None, :] >= rstart[:, None]) & (kr[None, :] < rstart[:, None] + WIN_H)
        dr = np.clip(kr[None, :] - r[:, None] + (WIN_H - 1), 0, 2 * WIN_H - 2)
        sel_r.append((dr[:, None, :] == np.arange(2 * WIN_H - 1)[None, :, None]).astype(np.float32))
        ok.append(ok_r[:, None, :, None] & ok_c[None, :, None, :])
    hi = lax.Precision.HIGHEST
    t = jnp.einsum("ciaj,lhab->lchijb", np.stack(sel_r), rpb, precision=hi)
    t = jnp.einsum("lchijb,qbk->lchiqjk", t, sel_c, precision=hi)
    t = jnp.where(np.stack(ok)[None, :, None], t, NEG_INF)
    return t.reshape(rpb.shape[0], 3, NA_HEADS, NA_QROWS * GRID_W, NA_KROWS * GRID_W)


def _swa_kernel(sink_ref, q_ref, k_ref, v_ref, o_ref, *, nblk, lc):
    i = pl.program_id(1)
    scale = HEAD_DIM ** -0.5
    blk = SWA_BLOCK
    rep = SWA_HEADS // SWA_KV_HEADS
    k_ctx = k_ref[0:lc, :]
    v_ctx = v_ref[0:lc, :]
    row = lax.broadcasted_iota(jnp.int32, (rep * blk, 1), 0)

    def attend(k_win, v_win, valid):
        outs = []
        scores = []
        for g in range(SWA_KV_HEADS):
            qq = jnp.concatenate([q_ref[:, (rep * g + r) * LANE:(rep * g + r + 1) * LANE] for r in range(rep)], axis=0)
            scores.append((_dot_nt(qq, k_ctx), None if k_win is None else _dot_nt(qq, k_win)))
        for g, (s_c, s_w) in enumerate(scores):
            sink = jnp.where(row < blk, sink_ref[rep * g], sink_ref[rep * g + 1])
            s_c = s_c * scale
            m = jnp.maximum(jnp.max(s_c, axis=-1, keepdims=True), sink)
            if k_win is not None:
                s_w = jnp.where(valid, s_w * scale, NEG_INF)
                m = jnp.maximum(m, jnp.max(s_w, axis=-1, keepdims=True))
                p_w = jnp.exp(s_w - m)
            p_c = jnp.exp(s_c - m)
            den = jnp.sum(p_c, axis=-1, keepdims=True) + jnp.exp(sink - m)
            o = _dot(p_c.astype(BF16), v_ctx)
            if k_win is not None:
                den = den + jnp.sum(p_w, axis=-1, keepdims=True)
                o = o + _dot(p_w.astype(BF16), v_win)
            outs.append(o / den)
        lane = lax.broadcasted_iota(jnp.int32, (blk, LANE), 1)
        lo = lane < HEAD_DIM
        o_ref[:, 0:LANE] = jnp.where(lo, outs[0][0:blk], outs[1][0:blk]).astype(o_ref.dtype)
        o_ref[:, LANE:2 * LANE] = jnp.where(lo, outs[0][blk:2 * blk], outs[1][blk:2 * blk]).astype(o_ref.dtype)

    n_ctx_blk = lc // blk

    @pl.when(i < n_ctx_blk)
    def _():
        attend(None, None, None)

    @pl.when(i >= n_ctx_blk)
    def _():
        n = i - n_ctx_blk
        wb = jnp.clip(n - 1, 0, nblk - 3)
        start = pl.multiple_of(lc + wb * blk, blk)
        iq = lax.broadcasted_iota(jnp.int32, (rep * blk, 3 * blk), 0) % blk
        ik = lax.broadcasted_iota(jnp.int32, (rep * blk, 3 * blk), 1)
        dist = (n - wb) * blk + iq - ik
        valid = jnp.abs(dist) <= SWA_WINDOW
        attend(k_ref[pl.ds(start, 3 * blk), :], v_ref[pl.ds(start, 3 * blk), :], valid)


def _swa_attention(pb, sink, lc):
    bsz, n_tok, _ = pb.shape
    blk = SWA_BLOCK
    nblk = (n_tok - lc) // blk
    qw = SWA_HEADS * LANE
    kw = SWA_KV_HEADS * HEAD_DIM
    return pl.pallas_call(
        functools.partial(_swa_kernel, nblk=nblk, lc=lc),
        grid=(bsz, n_tok // blk),
        in_specs=[
            pl.BlockSpec(memory_space=pltpu.SMEM),
            pl.BlockSpec((None, blk, qw), lambda b, i: (b, i, 0)),
            pl.BlockSpec((None, n_tok, kw), lambda b, i: (b, 0, qw // kw)),
            pl.BlockSpec((None, n_tok, kw), lambda b, i: (b, 0, qw // kw + 1)),
        ],
        out_specs=pl.BlockSpec((None, blk, SWA_HEADS * HEAD_DIM), lambda b, i: (b, i, 0)),
        out_shape=jax.ShapeDtypeStruct((bsz, n_tok, SWA_HEADS * HEAD_DIM), BF16),
        compiler_params=_params(2),
        name="swa_attention",
    )(sink, pb, pb, pb)


def _mla_chunk(n_tok):
    return next(c for c in (3 * TOK_TILE, 2 * TOK_TILE, TOK_TILE) if n_tok % c == 0)


def _reduce_rows(x, fn):
    n, w = x.shape
    for k in (8, 4, 2):
        while n > 8 and n % (8 * k) == 0:
            x = fn(x.reshape(k, n // k, w), axis=0)
            n //= k
    return fn(x, axis=0, keepdims=True)


def _mla_kernel(q_ref, k_ref, vt_ref, o_ref, qbd_ref, sa_ref, sb_ref, *, lc):
    i = pl.program_id(1)
    n_chunks, _, tk = vt_ref.shape
    tq = q_ref.shape[0]
    pairs = MLA_HEADS // 2

    lane = lax.broadcasted_iota(jnp.int32, (tq, 2 * LANE), 1)
    for j in range(pairs):
        qp = q_ref[:, 2 * j * LANE:2 * (j + 1) * LANE]
        qbd_ref[j, 0:tq, :] = jnp.where(lane < LANE, qp, jnp.zeros_like(qp))
        qbd_ref[j, tq:2 * tq, :] = jnp.where(lane >= LANE, qp, jnp.zeros_like(qp))

    def attend(n_trips, width):
        assert n_trips % 2 == 1

        def scores(c, buf):
            st = pl.multiple_of(c * tk, tk)
            maxima = []
            for j in range(pairs):
                s = _dot_nt(k_ref[pl.ds(st, width), 2 * j * LANE:2 * (j + 1) * LANE], qbd_ref[j])
                buf[j, 0:width, :] = s
                maxima.append(_reduce_rows(s, jnp.max))
            return tuple(maxima)

        def update(c, buf, maxima, state):
            new = []
            for j, (m_old, den, acc_a, acc_b) in enumerate(state):
                m_new = jnp.maximum(m_old, maxima[j])
                alpha = jnp.exp2(m_old - m_new)
                p = jnp.exp2(buf[j, 0:width, :] - m_new)
                den = alpha * den + _reduce_rows(p, jnp.sum)
                pb = p.astype(BF16)
                pv_a = _dot(vt_ref[c, 2 * j * MLA_V:(2 * j + 1) * MLA_V, 0:width], pb[:, 0:tq])
                pv_b = _dot(vt_ref[c, (2 * j + 1) * MLA_V:(2 * j + 2) * MLA_V, 0:width], pb[:, tq:2 * tq])
                acc_a = alpha[:, 0:tq] * acc_a + pv_a
                acc_b = alpha[:, tq:2 * tq] * acc_b + pv_b
                new.append((m_new, den, acc_a, acc_b))
            return tuple(new)

        def body(t, carry):
            maxima, state = carry
            nxt = scores(2 * t + 1, sb_ref)
            state = update(2 * t, sa_ref, maxima, state)
            maxima = scores(2 * t + 2, sa_ref)
            state = update(2 * t + 1, sb_ref, nxt, state)
            return maxima, state

        init = (jnp.full((1, 2 * tq), NEG_INF, F32), jnp.zeros((1, 2 * tq), F32),
                jnp.zeros((MLA_V, tq), F32), jnp.zeros((MLA_V, tq), F32))
        maxima, state = lax.fori_loop(0, n_trips // 2, body, (scores(0, sa_ref), (init,) * pairs))
        state = update(n_trips - 1, sa_ref, maxima, state)
        out = []
        for _, den, acc_a, acc_b in state:
            out += [acc_a / den[:, 0:tq], acc_b / den[:, tq:2 * tq]]
        o_ref[...] = jnp.concatenate(out, axis=0).T.astype(o_ref.dtype)

    @pl.when(i == 0)
    def _():
        attend(1, lc)

    @pl.when(i > 0)
    def _():
        attend(n_chunks, tk)


def _mla_attention(qm, km, vt, lc):
    bsz, n_tok, qw = qm.shape
    tq = TOK_TILE
    assert lc == tq
    wv = MLA_HEADS * MLA_V
    tk = vt.shape[-1]
    return pl.pallas_call(
        functools.partial(_mla_kernel, lc=lc),
        grid=(bsz, n_tok // tq),
        in_specs=[
            pl.BlockSpec((None, tq, qw), lambda b, i: (b, i, 0)),
            pl.BlockSpec((None, n_tok, qw), lambda b, i: (b, 0, 0)),
            pl.BlockSpec((None, n_tok // tk, wv, tk), lambda b, i: (b, 0, 0, 0)),
        ],
        out_specs=pl.BlockSpec((None, tq, wv), lambda b, i: (b, i, 0)),
        out_shape=jax.ShapeDtypeStruct((bsz, n_tok, wv), BF16),
        scratch_shapes=[pltpu.VMEM((MLA_HEADS // 2, 2 * tq, 2 * LANE), BF16),
                        pltpu.VMEM((MLA_HEADS // 2, tk, 2 * tq), F32), pltpu.VMEM((MLA_HEADS // 2, tk, 2 * tq), F32)],
        compiler_params=_params(2),
        name="mla_attention",
    )(qm, km, vt.reshape(bsz, n_tok // tk, wv, tk))


def _ssd_chunk(step, reverse, n_ctx, n_chunks):
    if not reverse:
        return step
    return jnp.where(step < n_ctx, n_ctx - 1 - step, n_chunks + n_ctx - 1 - step)


def _expand_heads(v, base):
    q = v.shape[0]
    lane = lax.broadcasted_iota(jnp.int32, (q, SSD_INNER), 1)
    out = jnp.broadcast_to(v[:, base:base + 1], (q, SSD_INNER))
    for h in range(1, SSD_HEADS):
        out = jnp.where(lane >= h * HEAD_DIM, jnp.broadcast_to(v[:, base + h:base + h + 1], (q, SSD_INNER)), out)
    return out


def _ssd_direction(ext_ref, dt_ref, cw_ref, cb_ref, dtb_ref, alog_ref, hs, reverse):
    q = SSD_CHUNK
    pad = SSD_CONV // 2
    base = SSD_HEADS if reverse else 0
    last = 0 if reverse else q - 1

    acc = ext_ref[pl.ds(SSD_HALO - pad, q), :] * cw_ref[0:1, :]
    for t in range(1, SSD_CONV):
        acc = acc + ext_ref[pl.ds(SSD_HALO - pad + t, q), :] * cw_ref[t:t + 1, :]
    xbc = _silu(acc + cb_ref[...])
    x = xbc[:, 0:SSD_INNER]

    dt_in = dt_ref[...] + dtb_ref[...]
    dt = jnp.maximum(dt_in, 0.0) + jnp.log1p(jnp.exp(-jnp.abs(dt_in)))
    da = dt * (-jnp.exp(alog_ref[...]))
    ii = lax.broadcasted_iota(jnp.int32, (q, q), 0)
    jj = lax.broadcasted_iota(jnp.int32, (q, q), 1)
    lower = (jj <= ii).astype(F32)
    upper = (jj >= ii).astype(F32)
    tri, tri_t = (upper, lower) if reverse else (lower, upper)
    cum = _dot_exact(tri, da)
    cum_t = _dot_exact(da.T, tri_t)
    causal = (jj >= ii) if reverse else (jj <= ii)

    dtx = _expand_heads(dt, base)
    cum_x = _expand_heads(cum, base)
    xdt = x * dtx
    xdt_b = xdt.astype(BF16)
    cum_last = cum_x[last:last + 1, :]
    xdtd_b = (xdt * jnp.exp(cum_last - cum_x)).astype(BF16)
    lane = lax.broadcasted_iota(jnp.int32, (q, SSD_INNER), 1)
    hs_b = hs.astype(BF16)

    y = jnp.zeros((q, SSD_INNER), F32)
    y_off = []
    new_state = []
    for g in range(SSD_GROUPS):
        bm = xbc[:, SSD_INNER + g * SSD_STATE:SSD_INNER + (g + 1) * SSD_STATE]
        cm = xbc[:, SSD_INNER + (SSD_GROUPS + g) * SSD_STATE:SSD_INNER + (SSD_GROUPS + g + 1) * SSD_STATE]
        cm_b = cm.astype(BF16)
        cb = _dot_nt(cm_b, bm.astype(BF16))
        for h in range(g * SSD_HEADS // SSD_GROUPS, (g + 1) * SSD_HEADS // SSD_GROUPS):
            col = cum[:, base + h:base + h + 1]
            rowv = cum_t[base + h:base + h + 1, :]
            seg = jnp.exp(jnp.where(causal, col - rowv, -jnp.inf))
            yd = _dot((cb * seg).astype(BF16), xdt_b)
            y = jnp.where((lane // HEAD_DIM) == h, yd, y)
        y_off.append(_dot(cm_b, hs_b))
        new_state.append(_dot(bm.T.astype(BF16), xdtd_b))
    half = lane < SSD_INNER // SSD_GROUPS
    y = y + jnp.where(half, y_off[0], y_off[1]) * jnp.exp(cum_x)
    hs_new = hs * jnp.exp(cum_last) + jnp.where(half, new_state[0], new_state[1])
    return y, x, hs_new


def _ssd_kernel(xf_ref, pf_ref, nf_ref, dtf_ref, xb_ref, pb_ref, nb_ref, dtr_ref,
                cw_ref, cb_ref, dtb_ref, alog_ref, skip_ref, yf_ref, yb_ref,
                extf_ref, extb_ref, hsf_ref, hsb_ref, *, n_ctx, n_chunks):
    step = pl.program_id(1)
    q = SSD_CHUNK

    @pl.when(step == 0)
    def _():
        hsf_ref[...] = jnp.zeros_like(hsf_ref)
        hsb_ref[...] = jnp.zeros_like(hsb_ref)

    for reverse, ext_ref, x_ref, p_ref, n_ref in ((False, extf_ref, xf_ref, pf_ref, nf_ref),
                                                  (True, extb_ref, xb_ref, pb_ref, nb_ref)):
        chunk = _ssd_chunk(step, reverse, n_ctx, n_chunks)
        has_prev = jnp.logical_and(chunk != 0, chunk != n_ctx)
        has_next = jnp.logical_and(chunk != n_ctx - 1, chunk != n_chunks - 1)
        ext_ref[0:SSD_HALO, :] = jnp.where(has_prev, p_ref[...], 0.0)
        ext_ref[SSD_HALO:SSD_HALO + q, :] = x_ref[...]
        ext_ref[SSD_HALO + q:, :] = jnp.where(has_next, n_ref[...], 0.0)

    params = (cw_ref, cb_ref, dtb_ref, alog_ref)
    y_f, x_f, hs_f = _ssd_direction(extf_ref, dtf_ref, *params, hsf_ref[...], False)
    y_b, _, hs_b = _ssd_direction(extb_ref, dtr_ref, *params, hsb_ref[...], True)
    yf_ref[...] = x_f * skip_ref[...] + y_f
    yb_ref[...] = y_b
    hsf_ref[...] = hs_f
    hsb_ref[...] = hs_b


def _ssd_scan(pxbc, dtp, conv_w, conv_b, dt_bias, a_log, skip, lc):
    bsz, n_tok, _ = pxbc.shape
    q = SSD_CHUNK
    n_chunks = n_tok // q
    n_ctx = lc // q
    per = q // SSD_HALO
    full = lambda a: pl.BlockSpec(a.shape, lambda b, s: (0,) * a.ndim)
    in_specs, args, out_specs = [], [], []
    for reverse in (False, True):
        chunk = functools.partial(_ssd_chunk, reverse=reverse, n_ctx=n_ctx, n_chunks=n_chunks)
        tok = lambda n, chunk=chunk: pl.BlockSpec((None, q, n), lambda b, s: (b, chunk(s), 0))
        prev = pl.BlockSpec((None, SSD_HALO, SSD_CONV_CH),
                            lambda b, s, chunk=chunk: (b, jnp.maximum(chunk(s) * per - 1, 0), 0))
        nxt = pl.BlockSpec((None, SSD_HALO, SSD_CONV_CH),
                           lambda b, s, chunk=chunk: (b, jnp.minimum((chunk(s) + 1) * per, n_tok // SSD_HALO - 1), 0))
        in_specs += [tok(SSD_CONV_CH), prev, nxt, tok(LANE)]
        args += [pxbc, pxbc, pxbc, dtp]
        out_specs.append(tok(SSD_INNER))
    in_specs += [full(conv_w), full(conv_b), full(dt_bias), full(a_log), full(skip)]
    args += [conv_w, conv_b, dt_bias, a_log, skip]
    ext = pltpu.VMEM((q + 2 * SSD_HALO, SSD_CONV_CH), F32)
    state = pltpu.VMEM((SSD_STATE, SSD_INNER), F32)
    return pl.pallas_call(
        functools.partial(_ssd_kernel, n_ctx=n_ctx, n_chunks=n_chunks),
        grid=(bsz, n_chunks),
        in_specs=in_specs,
        out_specs=out_specs,
        out_shape=[jax.ShapeDtypeStruct((bsz, n_tok, SSD_INNER), F32)] * 2,
        scratch_shapes=[ext, ext, state, state],
        compiler_params=_params(2),
        name="ssd_scan",
    )(*args)


def _outmlp_kernel(h_ref, oa_ref, ob_ref, om_ref, yf_ref, yb_ref, z_ref, gn_ref, moda_ref, modb_ref, wo_ref, g2_ref,
                   w1_ref, w2_ref, o_ref):
    def per_tile(fn):
        lo, hi = slice(0, TOK_TILE), slice(TOK_TILE, 2 * TOK_TILE)
        return jnp.concatenate([fn(lo, moda_ref), fn(hi, modb_ref)], axis=0)

    od = _rms((yf_ref[...] + yb_ref[...]) * _silu(z_ref[...]), gn_ref[...]).astype(BF16)
    mix = jnp.concatenate([oa_ref[...], ob_ref[...], om_ref[...], od], axis=1)
    attn = _dot(mix, wo_ref[...])
    h1 = per_tile(lambda r, m: h_ref[r, :] + m[2:3, :] * attn[r, :])
    xn = _rms(h1, g2_ref[...])
    xm = per_tile(lambda r, m: xn[r, :] * (1.0 + m[4:5, :]) + m[3:4, :]).astype(BF16)
    acc = jnp.zeros(h1.shape, F32)
    for c in range(D_FF // FF_TILE):
        a = jnp.maximum(_dot(xm, w1_ref[:, c * FF_TILE:(c + 1) * FF_TILE]), 0.0)
        acc = acc + _dot((a * a).astype(BF16), w2_ref[c * FF_TILE:(c + 1) * FF_TILE, :])
    o_ref[...] = per_tile(lambda r, m: h1[r, :] + m[5:6, :] * acc[r, :])


def _outmlp(h, oa, ob, om, yf, yb, pz, gnorm, mods, wo, g2, w1, w2, l, tiles_per_batch):
    n_tok = h.shape[0]
    tm = 2 * TOK_TILE
    assert n_tok % tm == 0
    tok = lambda n: pl.BlockSpec((tm, n), lambda t: (t, 0))
    full = lambda a: pl.BlockSpec(a.shape, lambda t: (0,) * a.ndim)
    mod = lambda k: pl.BlockSpec((None, 6, D_MODEL), lambda t: (_mod_row(2 * t + k, tiles_per_batch), 0, 0))
    once = pl.Buffered(1)
    return pl.pallas_call(
        _outmlp_kernel,
        grid=(n_tok // tm,),
        in_specs=[
            tok(D_MODEL), tok(GROUP_W), tok(GROUP_W), tok(GROUP_W), tok(GROUP_W), tok(GROUP_W), tok(GROUP_W),
            full(gnorm), mod(0), mod(1),
            _layer_spec(wo, l, once), full(g2), _layer_spec(w1, l, once), _layer_spec(w2, l, once),
        ],
        out_specs=tok(D_MODEL),
        out_shape=jax.ShapeDtypeStruct((n_tok, D_MODEL), F32),
        compiler_params=_params(1),
        name="outproj_mlp",
    )(h, oa, ob, om, yf, yb, pz, gnorm, mods, mods, wo, g2, w1, w2)


def _final_norm_kernel(h_ref, g_ref, o_ref):
    o_ref[...] = _rms(h_ref[...], g_ref[...])


def _final_norm(h, g, lc):
    bsz, n_tok, d = h.shape
    tm = TOK_TILE
    off = lc // tm
    return pl.pallas_call(
        _final_norm_kernel,
        grid=(bsz, (n_tok - lc) // tm),
        in_specs=[pl.BlockSpec((None, tm, d), lambda b, t: (b, t + off, 0)), pl.BlockSpec((1, d), lambda b, t: (0, 0))],
        out_specs=pl.BlockSpec((None, tm, d), lambda b, t: (b, t, 0)),
        out_shape=jax.ShapeDtypeStruct((bsz, n_tok - lc, d), F32),
        compiler_params=_params(2),
        name="final_norm",
    )(h, g)


def _rope_tables(s, lc):
    t = jnp.arange(s)
    pos = (t // GRID_W).astype(F32), (t % GRID_W).astype(F32)
    lane = jnp.arange(LANE)

    def table(d, width, active):
        nf = width // 4
        inv = 1.0 / (ROPE_BASE ** (jnp.arange(nf, dtype=F32) / nf))
        half = d // (width // 2)
        sub = (d % (width // 2)) // nf
        f = d % nf
        ang = jnp.where(half[None, :] == 0, pos[0][:, None], pos[1][:, None]) * inv[f][None, :]
        cos = jnp.where(active[None, :], jnp.cos(ang), 1.0)
        sin = jnp.where(active[None, :], jnp.where(sub[None, :] == 0, -1.0, 1.0) * jnp.sin(ang), 0.0)
        ident = jnp.ones((lc, LANE), F32), jnp.zeros((lc, LANE), F32)
        return jnp.concatenate([ident[0], cos]), jnp.concatenate([ident[1], sin])

    scos, ssin = table(lane % HEAD_DIM, HEAD_DIM, jnp.ones((LANE,), bool))
    m_act = (lane >= MLA_NOPE) & (lane < MLA_NOPE + MLA_ROPE)
    mcos, msin = table(jnp.clip(lane - MLA_NOPE, 0, MLA_ROPE - 1), MLA_ROPE, m_act)
    return scos, ssin, mcos, msin


def _extend_w_in(w_in):
    depth, d, _ = w_in.shape
    z = lambda n: jnp.zeros((depth, d, n), w_in.dtype)
    o_swa, o_mla, o_ssd = NA_IN, NA_IN + SWA_IN, NA_IN + SWA_IN + MLA_IN
    cols = [w_in[..., 0:NA_IN]]
    rep = SWA_HEADS // SWA_KV_HEADS
    for hq in range(SWA_HEADS):
        qh = w_in[..., o_swa + hq * HEAD_DIM:o_swa + (hq + 1) * HEAD_DIM]
        cols += [qh, z(HEAD_DIM)] if hq // rep == 0 else [z(HEAD_DIM), qh]
    cols.append(w_in[..., o_swa + SWA_HEADS * HEAD_DIM:o_mla])
    cols.append(w_in[..., o_mla:o_mla + MLA_Q_LORA + MLA_KV_LORA])
    kr = w_in[..., o_mla + MLA_Q_LORA + MLA_KV_LORA:o_ssd]
    for _ in range(MLA_HEADS):
        cols += [z(MLA_NOPE), kr, z(LANE - MLA_NOPE - MLA_ROPE)]
    cols.append(w_in[..., o_ssd:o_ssd + SSD_INNER + SSD_CONV_CH])
    cols += [w_in[..., o_ssd + SSD_INNER + SSD_CONV_CH:], z(LANE - 2 * SSD_HEADS)]
    w = jnp.concatenate(cols, axis=-1).astype(BF16)
    assert w.shape[-1] == N_EXT
    return w


def _mla_weights(w_uq, w_ukv):
    depth = w_uq.shape[0]
    dq = MLA_NOPE + MLA_ROPE
    uq, uk, uv = [], [], []
    for h in range(MLA_HEADS):
        uq += [w_uq[..., h * dq:(h + 1) * dq], jnp.zeros((depth, MLA_Q_LORA, LANE - dq), w_uq.dtype)]
        uk += [w_ukv[..., h * LANE:h * LANE + MLA_NOPE], jnp.zeros((depth, MLA_KV_LORA, LANE - MLA_NOPE), w_ukv.dtype)]
        uv.append(w_ukv[..., h * LANE + MLA_NOPE:(h + 1) * LANE])
    cat = lambda xs: jnp.concatenate(xs, axis=-1).astype(BF16)
    return cat(uq), cat(uk), cat(uv)


def _permute_w_out(w_out):
    blocks = [w_out[:, :GROUP_W]]
    for h in (0, 2, 1, 3):
        blocks.append(w_out[:, GROUP_W + h * HEAD_DIM:GROUP_W + (h + 1) * HEAD_DIM])
    blocks.append(w_out[:, 2 * GROUP_W:])
    return jnp.concatenate(blocks, axis=1).astype(BF16)


def kernel(x, c, ctx, c_ctx, w_mod, b_mod, g_norm1, w_in, na_rpb, swa_sink, mla_g_q, mla_g_kv, mla_w_uq, mla_w_ukv,
           ssd_conv_w, ssd_conv_b, ssd_dt_bias, ssd_a_log, ssd_d, ssd_g_norm, w_out, g_norm2, w_mlp1, w_mlp2, g_final):
    bsz, s, d = x.shape
    lc = ctx.shape[1]
    depth = w_in.shape[0]
    n_tok = lc + s
    rows = s // GRID_W
    tiles_per_batch = n_tok // TOK_TILE
    assert d == D_MODEL and lc == TOK_TILE and s % (NA_QROWS * GRID_W) == 0 and rows >= NA_KROWS + NA_QROWS
    assert bsz + 1 <= MOD_ROWS

    cvec = jnp.concatenate([c_ctx[None], c, jnp.zeros((MOD_ROWS - 1 - bsz, d), F32)], axis=0)
    mods = _modulation(cvec, w_mod, b_mod).reshape(depth, MOD_ROWS, 6, d)

    tabs = _rope_tables(s, lc)
    na_tables = _na_bias_tables(na_rpb, rows)
    w_ext = _extend_w_in(w_in)
    wuq, wk, wv = _mla_weights(mla_w_uq, mla_w_ukv)
    wo = _permute_w_out(w_out)
    w1 = w_mlp1.astype(BF16)
    w2 = w_mlp2.astype(BF16)
    pad_lanes = lambda a: jnp.pad(a.reshape(depth, 1, -1), ((0, 0), (0, 0), (0, LANE - a.shape[-1] * a.shape[-2])))
    dt_bias = pad_lanes(ssd_dt_bias)
    a_log = pad_lanes(ssd_a_log)
    conv_w = jnp.pad(ssd_conv_w, ((0, 0), (0, SSD_HALO - SSD_CONV), (0, 0)))
    skip = jnp.repeat(ssd_d, HEAD_DIM, axis=-1)

    h = jnp.concatenate([ctx, x], axis=1).reshape(bsz * n_tok, d)
    for l in range(depth):
        row = lambda a: a[l].reshape(1, -1)
        pa, pb, qm, km, vm, pz, pxbc, dtp = _inproj(
            h, mods[l], row(g_norm1), w_ext, l, tabs, row(mla_g_q), row(mla_g_kv), wuq[l], wk[l], wv[l],
            tiles_per_batch)
        per_batch = lambda a: a.reshape(bsz, n_tok, a.shape[-1])
        oa = _na_attention(per_batch(pa), na_tables, l, rows, lc)
        ob = _swa_attention(per_batch(pb), swa_sink[l], lc)
        om = _mla_attention(per_batch(qm), per_batch(km), vm, lc)
        yf, yb = _ssd_scan(per_batch(pxbc), per_batch(dtp), conv_w[l], row(ssd_conv_b), dt_bias[l], a_log[l],
                           row(skip), lc)
        flat = lambda a: a.reshape(bsz * n_tok, a.shape[-1])
        h = _outmlp(h, flat(oa), flat(ob), flat(om), flat(yf), flat(yb), pz, row(ssd_g_norm), mods[l], wo,
                    row(g_norm2), w1, w2, l, tiles_per_batch)
    return _final_norm(h.reshape(bsz, n_tok, d), g_final.reshape(1, d), lc)
```

```python
import functools
import math

import jax
import numpy as np
import jax.numpy as jnp
from jax import lax
from jax.experimental import pallas as pl
from jax.experimental.pallas import tpu as pltpu

F32 = jnp.float32
BF16 = jnp.bfloat16

D_MODEL = 1024
GRID_W = 64
HEAD_DIM = 64
D_FF = 4 * D_MODEL
NORM_EPS = 1e-6
ROPE_BASE = 10000.0
NEG_INF = -1e30
GROUP_W = D_MODEL // 4

NA_HEADS = 4
WIN_H = 8
WIN_W = 16
NA_QROWS = 4
NA_KROWS = 12

SWA_HEADS = 4
SWA_KV_HEADS = 2
SWA_WINDOW = 128
SWA_BLOCK = 128

MLA_HEADS = 4
MLA_Q_LORA = 256
MLA_KV_LORA = 128
MLA_NOPE = 64
MLA_ROPE = 32
MLA_V = 64
MLA_ONES_ROWS = 16
MLA_Q_SCALE = (MLA_NOPE + MLA_ROPE) ** -0.5 * math.log2(math.e)

SSD_INNER = GROUP_W
SSD_HEADS = 4
SSD_GROUPS = 2
SSD_STATE = 128
SSD_CONV = 5
SSD_CHUNK = 128
SSD_CONV_CH = SSD_INNER + 2 * SSD_GROUPS * SSD_STATE
SSD_HALO = 8

NA_IN = 3 * NA_HEADS * HEAD_DIM
SWA_IN = (SWA_HEADS + 2 * SWA_KV_HEADS) * HEAD_DIM
MLA_IN = MLA_Q_LORA + MLA_KV_LORA + MLA_ROPE
SSD_IN = 2 * SSD_INNER + 2 * SSD_GROUPS * SSD_STATE + 2 * SSD_HEADS

LANE = 128
TOK_TILE = 256
FF_TILE = 512
MOD_ROWS = 8
VMEM_LIMIT = 56 * 1024 * 1024

C_PA = 0
C_PB = C_PA + NA_IN
PB_W = SWA_HEADS * LANE + 2 * SWA_KV_HEADS * HEAD_DIM
C_CQ = C_PB + PB_W
C_CKV = C_CQ + MLA_Q_LORA
C_KR = C_CKV + MLA_KV_LORA
C_Z = C_KR + MLA_HEADS * LANE
C_XBC = C_Z + SSD_INNER
C_DT = C_XBC + SSD_CONV_CH
N_EXT = C_DT + LANE


def _dot(a, b):
    return jnp.dot(a, b, preferred_element_type=F32)


def _dot_nt(a, b):
    return lax.dot_general(a, b, (((1,), (1,)), ((), ())), preferred_element_type=F32)


def _dot_exact(a, b):
    return jnp.dot(a, b, preferred_element_type=F32, precision=lax.Precision.HIGHEST)


def _rms(x, g):
    return x * lax.rsqrt(jnp.mean(x * x, axis=-1, keepdims=True) + NORM_EPS) * g


def _silu(x):
    return x * jax.nn.sigmoid(x)


def _rope(x, cos, sin, half):
    n = x.shape[0]
    lane = lax.broadcasted_iota(jnp.int32, (n, LANE), 1)
    first = (lane % (2 * half)) < half
    outs = []
    for c in range(x.shape[1] // LANE):
        xc = x[:, c * LANE:(c + 1) * LANE]
        rot = jnp.where(first, pltpu.roll(xc, LANE - half, 1), pltpu.roll(xc, half, 1))
        outs.append(xc * cos + rot * sin)
    return outs[0] if len(outs) == 1 else jnp.concatenate(outs, axis=1)


def _params(n_axes):
    return pltpu.CompilerParams(dimension_semantics=("arbitrary",) * n_axes, vmem_limit_bytes=VMEM_LIMIT)


def _mod_kernel(c_ref, w_ref, b_ref, o_ref):
    o_ref[...] = _dot_exact(_silu(c_ref[...]), w_ref[...]) + b_ref[...]


def _modulation(cvec, w_mod, b_mod):
    depth, d, n6 = w_mod.shape
    tn = 1536
    return pl.pallas_call(
        _mod_kernel,
        grid=(depth, n6 // tn),
        in_specs=[
            pl.BlockSpec((MOD_ROWS, d), lambda l, j: (0, 0)),
            pl.BlockSpec((None, d, tn), lambda l, j: (l, 0, j)),
            pl.BlockSpec((None, 1, tn), lambda l, j: (l, 0, j)),
        ],
        out_specs=pl.BlockSpec((None, MOD_ROWS, tn), lambda l, j: (l, 0, j)),
        out_shape=jax.ShapeDtypeStruct((depth, MOD_ROWS, n6), F32),
        compiler_params=_params(2),
        name="modulation",
    )(cvec, w_mod, b_mod.reshape(depth, 1, n6))


def _mod_row(t, tiles_per_batch):
    return jnp.where(t % tiles_per_batch == 0, 0, 1 + t // tiles_per_batch)


def _inproj_kernel(h_ref, mod_ref, g1_ref, w_ref, scos_ref, ssin_ref, mcos_ref, msin_ref,
                   gq_ref, gkv_ref, wuq_ref, wk_ref, wv_ref,
                   pa_ref, pb_ref, qm_ref, km_ref, vm_ref, pz_ref, pxbc_ref, dt_ref):
    xn = _rms(h_ref[...], g1_ref[...])
    xm = (xn * (1.0 + mod_ref[1:2, :]) + mod_ref[0:1, :]).astype(BF16)

    def proj(lo, hi):
        return _dot(xm, w_ref[:, lo:hi])

    pa_ref[...] = proj(C_PA, C_PB).astype(BF16)

    n_qk = (SWA_HEADS + 1) * LANE
    qk = _rope(proj(C_PB, C_PB + n_qk), scos_ref[...], ssin_ref[...], HEAD_DIM // 4)
    pb_ref[:, 0:n_qk] = qk.astype(BF16)
    pb_ref[:, n_qk:PB_W] = proj(C_PB + n_qk, C_CQ).astype(BF16)

    mcos = mcos_ref[...]
    msin = msin_ref[...]
    cq = _rms(proj(C_CQ, C_CKV), gq_ref[...]).astype(BF16)
    qm_ref[...] = (_rope(_dot(cq, wuq_ref[...]), mcos, msin, MLA_ROPE // 4) * MLA_Q_SCALE).astype(BF16)
    ckv = _rms(proj(C_CKV, C_KR), gkv_ref[...]).astype(BF16)
    kr = _rope(proj(C_KR, C_Z), mcos, msin, MLA_ROPE // 4)
    km_ref[...] = (_dot(ckv, wk_ref[...]) + kr).astype(BF16)
    vm_ref[...] = _dot(ckv, wv_ref[...]).T.astype(BF16)

    pz_ref[...] = proj(C_Z, C_XBC)
    pxbc_ref[...] = proj(C_XBC, C_DT)
    dt_ref[...] = proj(C_DT, N_EXT)


def _layer_spec(a, l, mode=None):
    return pl.BlockSpec((None,) + a.shape[1:], lambda t: (l,) + (0,) * (a.ndim - 1), pipeline_mode=mode)


def _inproj(h, mods, g1, w_ext, l, tabs, gq, gkv, wuq, wk, wv, tiles_per_batch):
    n_tok = h.shape[0]
    tm = TOK_TILE
    scos, ssin, mcos, msin = tabs
    tok = lambda n: pl.BlockSpec((tm, n), lambda t: (t, 0))
    full = lambda a: pl.BlockSpec(a.shape, lambda t: (0,) * a.ndim)
    tab = pl.BlockSpec((tm, LANE), lambda t: (t % tiles_per_batch, 0))
    widths = (NA_IN, PB_W, MLA_HEADS * LANE, MLA_HEADS * LANE, None, SSD_INNER, SSD_CONV_CH, LANE)
    dtypes = (BF16, BF16, BF16, BF16, BF16, F32, F32, F32)
    tk = _mla_chunk(tiles_per_batch * tm)
    sub = tk // tm
    wv_out = MLA_HEADS * MLA_V
    vt_spec = pl.BlockSpec((None, wv_out, tm), lambda t: (t // sub, 0, t % sub))
    vt_shape = jax.ShapeDtypeStruct((n_tok // tk, wv_out, tk), BF16)
    return pl.pallas_call(
        _inproj_kernel,
        grid=(n_tok // tm,),
        in_specs=[
            tok(D_MODEL),
            pl.BlockSpec((None, 6, D_MODEL), lambda t: (_mod_row(t, tiles_per_batch), 0, 0)),
            full(g1), _layer_spec(w_ext, l), tab, tab, tab, tab, full(gq), full(gkv), full(wuq), full(wk), full(wv),
        ],
        out_specs=[vt_spec if n is None else tok(n) for n in widths],
        out_shape=[vt_shape if n is None else jax.ShapeDtypeStruct((n_tok, n), dt) for n, dt in zip(widths, dtypes)],
        compiler_params=_params(1),
        name="inproj",
    )(h, mods, g1, w_ext, scos, ssin, mcos, msin, gq, gkv, wuq, wk, wv)


def _na_kernel(q_ref, k_ref, v_ref, tab_ref, o_ref, *, rows, lc):
    i = pl.program_id(1)
    scale = HEAD_DIM ** -0.5
    nq = q_ref.shape[0]
    q = q_ref[...]
    lane = lax.broadcasted_iota(jnp.int32, (nq, NA_HEADS * HEAD_DIM), 1)
    k_ctx = k_ref[0:lc, :]
    v_ctx = v_ref[0:lc, :]

    def attend(k_win, v_win):
        out = jnp.zeros((nq, NA_HEADS * HEAD_DIM), F32)
        scores = []
        for h in range(NA_HEADS):
            qh = jnp.where((lane // HEAD_DIM) == h, q, jnp.zeros_like(q))
            scores.append((_dot_nt(qh, k_ctx), None if k_win is None else _dot_nt(qh, k_win)))
        for h, (s_c, s_w) in enumerate(scores):
            in_head = (lane // HEAD_DIM) == h
            s_c = s_c * scale
            m = jnp.max(s_c, axis=-1, keepdims=True)
            if k_win is not None:
                s_w = s_w * scale + tab_ref[h]
                m = jnp.maximum(m, jnp.max(s_w, axis=-1, keepdims=True))
                p_w = jnp.exp(s_w - m)
            p_c = jnp.exp(s_c - m)
            den = jnp.sum(p_c, axis=-1, keepdims=True)
            o = _dot(p_c.astype(BF16), v_ctx)
            if k_win is not None:
                den = den + jnp.sum(p_w, axis=-1, keepdims=True)
                o = o + _dot(p_w.astype(BF16), v_win)
            out = jnp.where(in_head, o / den, out)
        o_ref[...] = out.astype(o_ref.dtype)

    @pl.when(i == 0)
    def _():
        attend(None, None)

    @pl.when(i > 0)
    def _():
        r0 = (i - 1) * NA_QROWS
        ws = jnp.clip(r0 - WIN_H // 2, 0, rows - NA_KROWS)
        start = pl.multiple_of(lc + ws * GRID_W, GRID_W)
        nk = NA_KROWS * GRID_W
        attend(k_ref[pl.ds(start, nk), :], v_ref[pl.ds(start, nk), :])


def _na_attention(pa, tables, l, rows, lc):
    bsz, n_tok, _ = pa.shape
    nq = NA_QROWS * GRID_W
    groups = rows // NA_QROWS
    w = NA_HEADS * HEAD_DIM

    def cfg(b, i):
        g = i - 1
        return (l, jnp.where(g <= 0, 0, jnp.where(g == groups - 1, 2, 1)), 0, 0, 0)

    return pl.pallas_call(
        functools.partial(_na_kernel, rows=rows, lc=lc),
        grid=(bsz, 1 + groups),
        in_specs=[
            pl.BlockSpec((None, nq, w), lambda b, i: (b, i, 0)),
            pl.BlockSpec((None, n_tok, w), lambda b, i: (b, 0, 1)),
            pl.BlockSpec((None, n_tok, w), lambda b, i: (b, 0, 2)),
            pl.BlockSpec((None, None, NA_HEADS, nq, NA_KROWS * GRID_W), cfg),
        ],
        out_specs=pl.BlockSpec((None, nq, w), lambda b, i: (b, i, 0)),
        out_shape=jax.ShapeDtypeStruct((bsz, n_tok, w), BF16),
        compiler_params=_params(2),
        name="na_attention",
    )(pa, pa, pa, tables)


def _na_bias_tables(rpb, rows):
    groups = rows // NA_QROWS
    i = np.arange(NA_QROWS)
    j = np.arange(NA_KROWS)
    col = np.arange(GRID_W)
    cstart = np.clip(col - WIN_W // 2, 0, GRID_W - WIN_W)
    ok_c = (col[None, :] >= cstart[:, None]) & (col[None, :] < cstart[:, None] + WIN_W)
    dc = np.clip(col[None, :] - col[:, None] + (WIN_W - 1), 0, 2 * WIN_W - 2)
    sel_c = (dc[:, None, :] == np.arange(2 * WIN_W - 1)[None, :, None]).astype(np.float32)
    sel_r, ok = [], []
    for g in (0, 1, groups - 1):
        r = g * NA_QROWS + i
        kr = np.clip(g * NA_QROWS - WIN_H // 2, 0, rows - NA_KROWS) + j
        rstart = np.clip(r - WIN_H // 2, 0, rows - WIN_H)
        ok_r = (kr[None, :] >= rstart[:, None]) & (kr[None, :] < rstart[:, None] + WIN_H)
        dr = np.clip(kr[None, :] - r[:, None] + (WIN_H - 1), 0, 2 * WIN_H - 2)
        sel_r.append((dr[:, None, :] == np.arange(2 * WIN_H - 1)[None, :, None]).astype(np.float32))
        ok.append(ok_r[:, None, :, None] & ok_c[None, :, None, :])
    hi = lax.Precision.HIGHEST
    t = jnp.einsum("ciaj,lhab->lchijb", np.stack(sel_r), rpb, precision=hi)
    t = jnp.einsum("lchijb,qbk->lchiqjk", t, sel_c, precision=hi)
    t = jnp.where(np.stack(ok)[None, :, None], t, NEG_INF)
    return t.reshape(rpb.shape[0], 3, NA_HEADS, NA_QROWS * GRID_W, NA_KROWS * GRID_W)


def _swa_kernel(sink_ref, q_ref, k_ref, v_ref, o_ref, *, nblk, lc):
    i = pl.program_id(1)
    scale = HEAD_DIM ** -0.5
    blk = SWA_BLOCK
    rep = SWA_HEADS // SWA_KV_HEADS
    k_ctx = k_ref[0:lc, :]
    v_ctx = v_ref[0:lc, :]
    row = lax.broadcasted_iota(jnp.int32, (rep * blk, 1), 0)

    def attend(k_win, v_win, valid):
        outs = []
        scores = []
        for g in range(SWA_KV_HEADS):
            qq = jnp.concatenate([q_ref[:, (rep * g + r) * LANE:(rep * g + r + 1) * LANE] for r in range(rep)], axis=0)
            scores.append((_dot_nt(qq, k_ctx), None if k_win is None else _dot_nt(qq, k_win)))
        for g, (s_c, s_w) in enumerate(scores):
            sink = jnp.where(row < blk, sink_ref[rep * g], sink_ref[rep * g + 1])
            s_c = s_c * scale
            m = jnp.maximum(jnp.max(s_c, axis=-1, keepdims=True), sink)
            if k_win is not None:
                s_w = jnp.where(valid, s_w * scale, NEG_INF)
                m = jnp.maximum(m, jnp.max(s_w, axis=-1, keepdims=True))
                p_w = jnp.exp(s_w - m)
            p_c = jnp.exp(s_c - m)
            den = jnp.sum(p_c, axis=-1, keepdims=True) + jnp.exp(sink - m)
            o = _dot(p_c.astype(BF16), v_ctx)
            if k_win is not None:
                den = den + jnp.sum(p_w, axis=-1, keepdims=True)
                o = o + _dot(p_w.astype(BF16), v_win)
            outs.append(o / den)
        lane = lax.broadcasted_iota(jnp.int32, (blk, LANE), 1)
        lo = lane < HEAD_DIM
        o_ref[:, 0:LANE] = jnp.where(lo, outs[0][0:blk], outs[1][0:blk]).astype(o_ref.dtype)
        o_ref[:, LANE:2 * LANE] = jnp.where(lo, outs[0][blk:2 * blk], outs[1][blk:2 * blk]).astype(o_ref.dtype)

    n_ctx_blk = lc // blk

    @pl.when(i < n_ctx_blk)
    def _():
        attend(None, None, None)

    @pl.when(i >= n_ctx_blk)
    def _():
        n = i - n_ctx_blk
        wb = jnp.clip(n - 1, 0, nblk - 3)
        start = pl.multiple_of(lc + wb * blk, blk)
        iq = lax.broadcasted_iota(jnp.int32, (rep * blk, 3 * blk), 0) % blk
        ik = lax.broadcasted_iota(jnp.int32, (rep * blk, 3 * blk), 1)
        dist = (n - wb) * blk + iq - ik
        valid = jnp.abs(dist) <= SWA_WINDOW
        attend(k_ref[pl.ds(start, 3 * blk), :], v_ref[pl.ds(start, 3 * blk), :], valid)


def _swa_attention(pb, sink, lc):
    bsz, n_tok, _ = pb.shape
    blk = SWA_BLOCK
    nblk = (n_tok - lc) // blk
    qw = SWA_HEADS * LANE
    kw = SWA_KV_HEADS * HEAD_DIM
    return pl.pallas_call(
        functools.partial(_swa_kernel, nblk=nblk, lc=lc),
        grid=(bsz, n_tok // blk),
        in_specs=[
            pl.BlockSpec(memory_space=pltpu.SMEM),
            pl.BlockSpec((None, blk, qw), lambda b, i: (b, i, 0)),
            pl.BlockSpec((None, n_tok, kw), lambda b, i: (b, 0, qw // kw)),
            pl.BlockSpec((None, n_tok, kw), lambda b, i: (b, 0, qw // kw + 1)),
        ],
        out_specs=pl.BlockSpec((None, blk, SWA_HEADS * HEAD_DIM), lambda b, i: (b, i, 0)),
        out_shape=jax.ShapeDtypeStruct((bsz, n_tok, SWA_HEADS * HEAD_DIM), BF16),
        compiler_params=_params(2),
        name="swa_attention",
    )(sink, pb, pb, pb)


def _mla_chunk(n_tok):
    return next(c for c in (3 * TOK_TILE, 2 * TOK_TILE, TOK_TILE) if n_tok % c == 0)


def _reduce_rows(x, fn):
    n, w = x.shape
    for k in (8, 4, 2):
        while n > 8 and n % (8 * k) == 0:
            x = fn(x.reshape(k, n // k, w), axis=0)
            n //= k
    return fn(x, axis=0, keepdims=True)


def _mla_kernel(q_ref, k_ref, vt_ref, o_ref, qbd_ref, sa_ref, sb_ref, *, lc):
    i = pl.program_id(1)
    n_chunks, _, tk = vt_ref.shape
    tq = q_ref.shape[0]
    pairs = MLA_HEADS // 2

    lane = lax.broadcasted_iota(jnp.int32, (tq, 2 * LANE), 1)
    for j in range(pairs):
        qp = q_ref[:, 2 * j * LANE:2 * (j + 1) * LANE]
        qbd_ref[j, 0:tq, :] = jnp.where(lane < LANE, qp, jnp.zeros_like(qp))
        qbd_ref[j, tq:2 * tq, :] = jnp.where(lane >= LANE, qp, jnp.zeros_like(qp))

    def attend(n_trips, width):
        assert n_trips % 2 == 1

        def scores(c, buf):
            st = pl.multiple_of(c * tk, tk)
            maxima = []
            for j in range(pairs):
                s = _dot_nt(k_ref[pl.ds(st, width), 2 * j * LANE:2 * (j + 1) * LANE], qbd_ref[j])
                buf[j, 0:width, :] = s
                maxima.append(_reduce_rows(s, jnp.max))
            return tuple(maxima)

        def update(c, buf, maxima, state):
            new = []
            for j, (m_old, den, acc_a, acc_b) in enumerate(state):
                m_new = jnp.maximum(m_old, maxima[j])
                alpha = jnp.exp2(m_old - m_new)
                pb = jnp.exp2(buf[j, 0:width, :] - m_new).astype(BF16)
                ones = jnp.ones((MLA_ONES_ROWS, width), BF16)
                va = jnp.concatenate([vt_ref[c, 2 * j * MLA_V:(2 * j + 1) * MLA_V, 0:width], ones], axis=0)
                vb = jnp.concatenate([vt_ref[c, (2 * j + 1) * MLA_V:(2 * j + 2) * MLA_V, 0:width], ones], axis=0)
                pv_a = _dot(va, pb[:, 0:tq])
                pv_b = _dot(vb, pb[:, tq:2 * tq])
                sums = jnp.concatenate([pv_a[MLA_V:MLA_V + 1, :], pv_b[MLA_V:MLA_V + 1, :]], axis=1)
                den = alpha * den + sums
                acc_a = alpha[:, 0:tq] * acc_a + pv_a[0:MLA_V, :]
                acc_b = alpha[:, tq:2 * tq] * acc_b + pv_b[0:MLA_V, :]
                new.append((m_new, den, acc_a, acc_b))
            return tuple(new)

        def body(t, carry):
            maxima, state = carry
            nxt = scores(2 * t + 1, sb_ref)
            state = update(2 * t, sa_ref, maxima, state)
            maxima = scores(2 * t + 2, sa_ref)
            state = update(2 * t + 1, sb_ref, nxt, state)
            return maxima, state

        init = (jnp.full((1, 2 * tq), NEG_INF, F32), jnp.zeros((1, 2 * tq), F32),
                jnp.zeros((MLA_V, tq), F32), jnp.zeros((MLA_V, tq), F32))
        maxima, state = lax.fori_loop(0, n_trips // 2, body, (scores(0, sa_ref), (init,) * pairs))
        state = update(n_trips - 1, sa_ref, maxima, state)
        out = []
        for _, den, acc_a, acc_b in state:
            out += [acc_a / den[:, 0:tq], acc_b / den[:, tq:2 * tq]]
        o_ref[...] = jnp.concatenate(out, axis=0).T.astype(o_ref.dtype)

    @pl.when(i == 0)
    def _():
        attend(1, lc)

    @pl.when(i > 0)
    def _():
        attend(n_chunks, tk)


def _mla_attention(qm, km, vt, lc):
    bsz, n_tok, qw = qm.shape
    tq = TOK_TILE
    assert lc == tq
    wv = MLA_HEADS * MLA_V
    tk = vt.shape[-1]
    return pl.pallas_call(
        functools.partial(_mla_kernel, lc=lc),
        grid=(bsz, n_tok // tq),
        in_specs=[
            pl.BlockSpec((None, tq, qw), lambda b, i: (b, i, 0)),
            pl.BlockSpec((None, n_tok, qw), lambda b, i: (b, 0, 0)),
            pl.BlockSpec((None, n_tok // tk, wv, tk), lambda b, i: (b, 0, 0, 0)),
        ],
        out_specs=pl.BlockSpec((None, tq, wv), lambda b, i: (b, i, 0)),
        out_shape=jax.ShapeDtypeStruct((bsz, n_tok, wv), BF16),
        scratch_shapes=[pltpu.VMEM((MLA_HEADS // 2, 2 * tq, 2 * LANE), BF16),
                        pltpu.VMEM((MLA_HEADS // 2, tk, 2 * tq), F32), pltpu.VMEM((MLA_HEADS // 2, tk, 2 * tq), F32)],
        compiler_params=_params(2),
        name="mla_attention",
    )(qm, km, vt.reshape(bsz, n_tok // tk, wv, tk))


def _ssd_chunk(step, reverse, n_ctx, n_chunks):
    if not reverse:
        return step
    return jnp.where(step < n_ctx, n_ctx - 1 - step, n_chunks + n_ctx - 1 - step)


def _expand_heads(v, base):
    q = v.shape[0]
    lane = lax.broadcasted_iota(jnp.int32, (q, SSD_INNER), 1)
    out = jnp.broadcast_to(v[:, base:base + 1], (q, SSD_INNER))
    for h in range(1, SSD_HEADS):
        out = jnp.where(lane >= h * HEAD_DIM, jnp.broadcast_to(v[:, base + h:base + h + 1], (q, SSD_INNER)), out)
    return out


def _ssd_direction(ext_ref, dt_ref, cw_ref, cb_ref, dtb_ref, alog_ref, hs, reverse):
    q = SSD_CHUNK
    pad = SSD_CONV // 2
    base = SSD_HEADS if reverse else 0
    last = 0 if reverse else q - 1

    acc = ext_ref[pl.ds(SSD_HALO - pad, q), :] * cw_ref[0:1, :]
    for t in range(1, SSD_CONV):
        acc = acc + ext_ref[pl.ds(SSD_HALO - pad + t, q), :] * cw_ref[t:t + 1, :]
    xbc = _silu(acc + cb_ref[...])
    x = xbc[:, 0:SSD_INNER]

    dt_in = dt_ref[...] + dtb_ref[...]
    dt = jnp.maximum(dt_in, 0.0) + jnp.log1p(jnp.exp(-jnp.abs(dt_in)))
    da = dt * (-jnp.exp(alog_ref[...]))
    ii = lax.broadcasted_iota(jnp.int32, (q, q), 0)
    jj = lax.broadcasted_iota(jnp.int32, (q, q), 1)
    lower = (jj <= ii).astype(F32)
    upper = (jj >= ii).astype(F32)
    tri, tri_t = (upper, lower) if reverse else (lower, upper)
    cum = _dot_exact(tri, da)
    cum_t = _dot_exact(da.T, tri_t)
    causal = (jj >= ii) if reverse else (jj <= ii)

    dtx = _expand_heads(dt, base)
    cum_x = _expand_heads(cum, base)
    xdt = x * dtx
    xdt_b = xdt.astype(BF16)
    cum_last = cum_x[last:last + 1, :]
    xdtd_b = (xdt * jnp.exp(cum_last - cum_x)).astype(BF16)
    lane = lax.broadcasted_iota(jnp.int32, (q, SSD_INNER), 1)
    hs_b = hs.astype(BF16)

    y = jnp.zeros((q, SSD_INNER), F32)
    y_off = []
    new_state = []
    for g in range(SSD_GROUPS):
        bm = xbc[:, SSD_INNER + g * SSD_STATE:SSD_INNER + (g + 1) * SSD_STATE]
        cm = xbc[:, SSD_INNER + (SSD_GROUPS + g) * SSD_STATE:SSD_INNER + (SSD_GROUPS + g + 1) * SSD_STATE]
        cm_b = cm.astype(BF16)
        cb = _dot_nt(cm_b, bm.astype(BF16))
        for h in range(g * SSD_HEADS // SSD_GROUPS, (g + 1) * SSD_HEADS // SSD_GROUPS):
            col = cum[:, base + h:base + h + 1]
            rowv = cum_t[base + h:base + h + 1, :]
            seg = jnp.exp(jnp.where(causal, col - rowv, -jnp.inf))
            yd = _dot((cb * seg).astype(BF16), xdt_b)
            y = jnp.where((lane // HEAD_DIM) == h, yd, y)
        y_off.append(_dot(cm_b, hs_b))
        new_state.append(_dot(bm.T.astype(BF16), xdtd_b))
    half = lane < SSD_INNER // SSD_GROUPS
    y = y + jnp.where(half, y_off[0], y_off[1]) * jnp.exp(cum_x)
    hs_new = hs * jnp.exp(cum_last) + jnp.where(half, new_state[0], new_state[1])
    return y, x, hs_new


def _ssd_kernel(xf_ref, pf_ref, nf_ref, dtf_ref, xb_ref, pb_ref, nb_ref, dtr_ref,
                cw_ref, cb_ref, dtb_ref, alog_ref, skip_ref, yf_ref, yb_ref,
                extf_ref, extb_ref, hsf_ref, hsb_ref, *, n_ctx, n_chunks):
    step = pl.program_id(1)
    q = SSD_CHUNK

    @pl.when(step == 0)
    def _():
        hsf_ref[...] = jnp.zeros_like(hsf_ref)
        hsb_ref[...] = jnp.zeros_like(hsb_ref)

    for reverse, ext_ref, x_ref, p_ref, n_ref in ((False, extf_ref, xf_ref, pf_ref, nf_ref),
                                                  (True, extb_ref, xb_ref, pb_ref, nb_ref)):
        chunk = _ssd_chunk(step, reverse, n_ctx, n_chunks)
        has_prev = jnp.logical_and(chunk != 0, chunk != n_ctx)
        has_next = jnp.logical_and(chunk != n_ctx - 1, chunk != n_chunks - 1)
        ext_ref[0:SSD_HALO, :] = jnp.where(has_prev, p_ref[...], 0.0)
        ext_ref[SSD_HALO:SSD_HALO + q, :] = x_ref[...]
        ext_ref[SSD_HALO + q:, :] = jnp.where(has_next, n_ref[...], 0.0)

    params = (cw_ref, cb_ref, dtb_ref, alog_ref)
    y_f, x_f, hs_f = _ssd_direction(extf_ref, dtf_ref, *params, hsf_ref[...], False)
    y_b, _, hs_b = _ssd_direction(extb_ref, dtr_ref, *params, hsb_ref[...], True)
    yf_ref[...] = x_f * skip_ref[...] + y_f
    yb_ref[...] = y_b
    hsf_ref[...] = hs_f
    hsb_ref[...] = hs_b


def _ssd_scan(pxbc, dtp, conv_w, conv_b, dt_bias, a_log, skip, lc):
    bsz, n_tok, _ = pxbc.shape
    q = SSD_CHUNK
    n_chunks = n_tok // q
    n_ctx = lc // q
    per = q // SSD_HALO
    full = lambda a: pl.BlockSpec(a.shape, lambda b, s: (0,) * a.ndim)
    in_specs, args, out_specs = [], [], []
    for reverse in (False, True):
        chunk = functools.partial(_ssd_chunk, reverse=reverse, n_ctx=n_ctx, n_chunks=n_chunks)
        tok = lambda n, chunk=chunk: pl.BlockSpec((None, q, n), lambda b, s: (b, chunk(s), 0))
        prev = pl.BlockSpec((None, SSD_HALO, SSD_CONV_CH),
                            lambda b, s, chunk=chunk: (b, jnp.maximum(chunk(s) * per - 1, 0), 0))
        nxt = pl.BlockSpec((None, SSD_HALO, SSD_CONV_CH),
                           lambda b, s, chunk=chunk: (b, jnp.minimum((chunk(s) + 1) * per, n_tok // SSD_HALO - 1), 0))
        in_specs += [tok(SSD_CONV_CH), prev, nxt, tok(LANE)]
        args += [pxbc, pxbc, pxbc, dtp]
        out_specs.append(tok(SSD_INNER))
    in_specs += [full(conv_w), full(conv_b), full(dt_bias), full(a_log), full(skip)]
    args += [conv_w, conv_b, dt_bias, a_log, skip]
    ext = pltpu.VMEM((q + 2 * SSD_HALO, SSD_CONV_CH), F32)
    state = pltpu.VMEM((SSD_STATE, SSD_INNER), F32)
    return pl.pallas_call(
        functools.partial(_ssd_kernel, n_ctx=n_ctx, n_chunks=n_chunks),
        grid=(bsz, n_chunks),
        in_specs=in_specs,
        out_specs=out_specs,
        out_shape=[jax.ShapeDtypeStruct((bsz, n_tok, SSD_INNER), F32)] * 2,
        scratch_shapes=[ext, ext, state, state],
        compiler_params=_params(2),
        name="ssd_scan",
    )(*args)


def _outmlp_kernel(h_ref, oa_ref, ob_ref, om_ref, yf_ref, yb_ref, z_ref, gn_ref, moda_ref, modb_ref, wo_ref, g2_ref,
                   w1_ref, w2_ref, o_ref):
    def per_tile(fn):
        lo, hi = slice(0, TOK_TILE), slice(TOK_TILE, 2 * TOK_TILE)
        return jnp.concatenate([fn(lo, moda_ref), fn(hi, modb_ref)], axis=0)

    od = _rms((yf_ref[...] + yb_ref[...]) * _silu(z_ref[...]), gn_ref[...]).astype(BF16)
    mix = jnp.concatenate([oa_ref[...], ob_ref[...], om_ref[...], od], axis=1)
    attn = _dot(mix, wo_ref[...])
    h1 = per_tile(lambda r, m: h_ref[r, :] + m[2:3, :] * attn[r, :])
    xn = _rms(h1, g2_ref[...])
    xm = per_tile(lambda r, m: xn[r, :] * (1.0 + m[4:5, :]) + m[3:4, :]).astype(BF16)
    acc = jnp.zeros(h1.shape, F32)
    for c in range(D_FF // FF_TILE):
        a = jnp.maximum(_dot(xm, w1_ref[:, c * FF_TILE:(c + 1) * FF_TILE]), 0.0)
        acc = acc + _dot((a * a).astype(BF16), w2_ref[c * FF_TILE:(c + 1) * FF_TILE, :])
    o_ref[...] = per_tile(lambda r, m: h1[r, :] + m[5:6, :] * acc[r, :])


def _outmlp(h, oa, ob, om, yf, yb, pz, gnorm, mods, wo, g2, w1, w2, l, tiles_per_batch):
    n_tok = h.shape[0]
    tm = 2 * TOK_TILE
    assert n_tok % tm == 0
    tok = lambda n: pl.BlockSpec((tm, n), lambda t: (t, 0))
    full = lambda a: pl.BlockSpec(a.shape, lambda t: (0,) * a.ndim)
    mod = lambda k: pl.BlockSpec((None, 6, D_MODEL), lambda t: (_mod_row(2 * t + k, tiles_per_batch), 0, 0))
    once = pl.Buffered(1)
    return pl.pallas_call(
        _outmlp_kernel,
        grid=(n_tok // tm,),
        in_specs=[
            tok(D_MODEL), tok(GROUP_W), tok(GROUP_W), tok(GROUP_W), tok(GROUP_W), tok(GROUP_W), tok(GROUP_W),
            full(gnorm), mod(0), mod(1),
            _layer_spec(wo, l, once), full(g2), _layer_spec(w1, l, once), _layer_spec(w2, l, once),
        ],
        out_specs=tok(D_MODEL),
        out_shape=jax.ShapeDtypeStruct((n_tok, D_MODEL), F32),
        compiler_params=_params(1),
        name="outproj_mlp",
    )(h, oa, ob, om, yf, yb, pz, gnorm, mods, mods, wo, g2, w1, w2)


def _final_norm_kernel(h_ref, g_ref, o_ref):
    o_ref[...] = _rms(h_ref[...], g_ref[...])


def _final_norm(h, g, lc):
    bsz, n_tok, d = h.shape
    tm = TOK_TILE
    off = lc // tm
    return pl.pallas_call(
        _final_norm_kernel,
        grid=(bsz, (n_tok - lc) // tm),
        in_specs=[pl.BlockSpec((None, tm, d), lambda b, t: (b, t + off, 0)), pl.BlockSpec((1, d), lambda b, t: (0, 0))],
        out_specs=pl.BlockSpec((None, tm, d), lambda b, t: (b, t, 0)),
        out_shape=jax.ShapeDtypeStruct((bsz, n_tok - lc, d), F32),
        compiler_params=_params(2),
        name="final_norm",
    )(h, g)


def _rope_tables(s, lc):
    t = jnp.arange(s)
    pos = (t // GRID_W).astype(F32), (t % GRID_W).astype(F32)
    lane = jnp.arange(LANE)

    def table(d, width, active):
        nf = width // 4
        inv = 1.0 / (ROPE_BASE ** (jnp.arange(nf, dtype=F32) / nf))
        half = d // (width // 2)
        sub = (d % (width // 2)) // nf
        f = d % nf
        ang = jnp.where(half[None, :] == 0, pos[0][:, None], pos[1][:, None]) * inv[f][None, :]
        cos = jnp.where(active[None, :], jnp.cos(ang), 1.0)
        sin = jnp.where(active[None, :], jnp.where(sub[None, :] == 0, -1.0, 1.0) * jnp.sin(ang), 0.0)
        ident = jnp.ones((lc, LANE), F32), jnp.zeros((lc, LANE), F32)
        return jnp.concatenate([ident[0], cos]), jnp.concatenate([ident[1], sin])

    scos, ssin = table(lane % HEAD_DIM, HEAD_DIM, jnp.ones((LANE,), bool))
    m_act = (lane >= MLA_NOPE) & (lane < MLA_NOPE + MLA_ROPE)
    mcos, msin = table(jnp.clip(lane - MLA_NOPE, 0, MLA_ROPE - 1), MLA_ROPE, m_act)
    return scos, ssin, mcos, msin


def _extend_w_in(w_in):
    depth, d, _ = w_in.shape
    z = lambda n: jnp.zeros((depth, d, n), w_in.dtype)
    o_swa, o_mla, o_ssd = NA_IN, NA_IN + SWA_IN, NA_IN + SWA_IN + MLA_IN
    cols = [w_in[..., 0:NA_IN]]
    rep = SWA_HEADS // SWA_KV_HEADS
    for hq in range(SWA_HEADS):
        qh = w_in[..., o_swa + hq * HEAD_DIM:o_swa + (hq + 1) * HEAD_DIM]
        cols += [qh, z(HEAD_DIM)] if hq // rep == 0 else [z(HEAD_DIM), qh]
    cols.append(w_in[..., o_swa + SWA_HEADS * HEAD_DIM:o_mla])
    cols.append(w_in[..., o_mla:o_mla + MLA_Q_LORA + MLA_KV_LORA])
    kr = w_in[..., o_mla + MLA_Q_LORA + MLA_KV_LORA:o_ssd]
    for _ in range(MLA_HEADS):
        cols += [z(MLA_NOPE), kr, z(LANE - MLA_NOPE - MLA_ROPE)]
    cols.append(w_in[..., o_ssd:o_ssd + SSD_INNER + SSD_CONV_CH])
    cols += [w_in[..., o_ssd + SSD_INNER + SSD_CONV_CH:], z(LANE - 2 * SSD_HEADS)]
    w = jnp.concatenate(cols, axis=-1).astype(BF16)
    assert w.shape[-1] == N_EXT
    return w


def _mla_weights(w_uq, w_ukv):
    depth = w_uq.shape[0]
    dq = MLA_NOPE + MLA_ROPE
    uq, uk, uv = [], [], []
    for h in range(MLA_HEADS):
        uq += [w_uq[..., h * dq:(h + 1) * dq], jnp.zeros((depth, MLA_Q_LORA, LANE - dq), w_uq.dtype)]
        uk += [w_ukv[..., h * LANE:h * LANE + MLA_NOPE], jnp.zeros((depth, MLA_KV_LORA, LANE - MLA_NOPE), w_ukv.dtype)]
        uv.append(w_ukv[..., h * LANE + MLA_NOPE:(h + 1) * LANE])
    cat = lambda xs: jnp.concatenate(xs, axis=-1).astype(BF16)
    return cat(uq), cat(uk), cat(uv)


def _permute_w_out(w_out):
    blocks = [w_out[:, :GROUP_W]]
    for h in (0, 2, 1, 3):
        blocks.append(w_out[:, GROUP_W + h * HEAD_DIM:GROUP_W + (h + 1) * HEAD_DIM])
    blocks.append(w_out[:, 2 * GROUP_W:])
    return jnp.concatenate(blocks, axis=1).astype(BF16)


def kernel(x, c, ctx, c_ctx, w_mod, b_mod, g_norm1, w_in, na_rpb, swa_sink, mla_g_q, mla_g_kv, mla_w_uq, mla_w_ukv,
           ssd_conv_w, ssd_conv_b, ssd_dt_bias, ssd_a_log, ssd_d, ssd_g_norm, w_out, g_norm2, w_mlp1, w_mlp2, g_final):
    bsz, s, d = x.shape
    lc = ctx.shape[1]
    depth = w_in.shape[0]
    n_tok = lc + s
    rows = s // GRID_W
    tiles_per_batch = n_tok // TOK_TILE
    assert d == D_MODEL and lc == TOK_TILE and s % (NA_QROWS * GRID_W) == 0 and rows >= NA_KROWS + NA_QROWS
    assert bsz + 1 <= MOD_ROWS

    cvec = jnp.concatenate([c_ctx[None], c, jnp.zeros((MOD_ROWS - 1 - bsz, d), F32)], axis=0)
    mods = _modulation(cvec, w_mod, b_mod).reshape(depth, MOD_ROWS, 6, d)

    tabs = _rope_tables(s, lc)
    na_tables = _na_bias_tables(na_rpb, rows)
    w_ext = _extend_w_in(w_in)
    wuq, wk, wv = _mla_weights(mla_w_uq, mla_w_ukv)
    wo = _permute_w_out(w_out)
    w1 = w_mlp1.astype(BF16)
    w2 = w_mlp2.astype(BF16)
    pad_lanes = lambda a: jnp.pad(a.reshape(depth, 1, -1), ((0, 0), (0, 0), (0, LANE - a.shape[-1] * a.shape[-2])))
    dt_bias = pad_lanes(ssd_dt_bias)
    a_log = pad_lanes(ssd_a_log)
    conv_w = jnp.pad(ssd_conv_w, ((0, 0), (0, SSD_HALO - SSD_CONV), (0, 0)))
    skip = jnp.repeat(ssd_d, HEAD_DIM, axis=-1)

    h = jnp.concatenate([ctx, x], axis=1).reshape(bsz * n_tok, d)
    for l in range(depth):
        row = lambda a: a[l].reshape(1, -1)
        pa, pb, qm, km, vm, pz, pxbc, dtp = _inproj(
            h, mods[l], row(g_norm1), w_ext, l, tabs, row(mla_g_q), row(mla_g_kv), wuq[l], wk[l], wv[l],
            tiles_per_batch)
        per_batch = lambda a: a.reshape(bsz, n_tok, a.shape[-1])
        oa = _na_attention(per_batch(pa), na_tables, l, rows, lc)
        ob = _swa_attention(per_batch(pb), swa_sink[l], lc)
        om = _mla_attention(per_batch(qm), per_batch(km), vm, lc)
        yf, yb = _ssd_scan(per_batch(pxbc), per_batch(dtp), conv_w[l], row(ssd_conv_b), dt_bias[l], a_log[l],
                           row(skip), lc)
        flat = lambda a: a.reshape(bsz * n_tok, a.shape[-1])
        h = _outmlp(h, flat(oa), flat(ob), flat(om), flat(yf), flat(yb), pz, row(ssd_g_norm), mods[l], wo,
                    row(g_norm2), w1, w2, l, tiles_per_batch)
    return _final_norm(h.reshape(bsz, n_tok, d), g_final.reshape(1, d), lc)
```
